```python
import math
import jax
import jax.numpy as jnp
from jax import lax
import numpy as np

D_MODEL = 2048
BATCH = 4
SEQ = 2048
DEPTH = 1
DEC_BATCH = 32
DEC_SEQ = 1
PAST_LEN = 16384
PAGE_SIZE = 128

HEAD_DIM = 128
N_HEADS_A = 8
N_KV_A = 4
N_HEADS_B = 8
N_KV_B = 4
MOBA_BLOCK = 256
MOBA_TOPK = 3
MOBA_Q_CHUNK = 8
IDX_HEADS = 16
IDX_DIM = 128
DSA_TOPK = 256
DSA_Q_CHUNK = 64
PEER_HEADS = 8
PEER_KEY_DIM = 256
PEER_HALF = PEER_KEY_DIM // 2
PEER_N_KEYS = 128
PEER_N_EXPERTS = PEER_N_KEYS * PEER_N_KEYS
PEER_TOPK = 16
PEER_CHUNK = 128
PLE_DIM = 256
ROPE_THETA = 10000.0
RMS_EPS = 1e-6
_SPLITS = (N_HEADS_A * HEAD_DIM, N_KV_A * HEAD_DIM, N_KV_A * HEAD_DIM,
           N_HEADS_B * HEAD_DIM, N_KV_B * HEAD_DIM, N_KV_B * HEAD_DIM,
           IDX_HEADS * IDX_DIM, IDX_DIM, IDX_HEADS, D_MODEL, D_MODEL)
IN_COLS = sum(_SPLITS)

kernel_name = 'hybrid_moba_dsa_peer_step'


def rmsnorm(x, g):
    xf = x.astype(jnp.float32)
    xf = xf * lax.rsqrt(jnp.mean(xf * xf, axis=-1, keepdims=True) + RMS_EPS)
    return xf.astype(x.dtype) * g


def rope(x, pos):
    half = x.shape[-1] // 2
    inv = ROPE_THETA ** (-jnp.arange(half, dtype=jnp.float32) / half)
    ang = pos.astype(jnp.float32)[:, None] * inv[None, :]
    cos = jnp.cos(ang)[None, :, None, :].astype(x.dtype)
    sin = jnp.sin(ang)[None, :, None, :].astype(x.dtype)
    x1, x2 = x[..., :half], x[..., half:]
    return jnp.concatenate([x1 * cos - x2 * sin, x2 * cos + x1 * sin], axis=-1)


def project_mixers(h, pos, w_in):
    B, T, _ = h.shape
    offsets = np.cumsum(np.array(_SPLITS))[:-1].tolist()
    q_a, k_a, v_a, q_b, k_b, v_b, qi, ki, wi, ga, gb = jnp.split(h @ w_in, offsets, axis=-1)
    q_a = rope(q_a.reshape(B, T, N_HEADS_A, HEAD_DIM), pos)
    k_a = rope(k_a.reshape(B, T, N_KV_A, HEAD_DIM), pos)
    v_a = v_a.reshape(B, T, N_KV_A, HEAD_DIM)
    q_b = rope(q_b.reshape(B, T, N_HEADS_B, HEAD_DIM), pos)
    k_b = rope(k_b.reshape(B, T, N_KV_B, HEAD_DIM), pos)
    v_b = v_b.reshape(B, T, N_KV_B, HEAD_DIM)
    qi = rope(qi.reshape(B, T, IDX_HEADS, IDX_DIM), pos)
    ki = rope(ki[:, :, None, :], pos)[:, :, 0]
    wi = wi * (IDX_HEADS * IDX_DIM) ** -0.5
    return q_a, k_a, v_a, q_b, k_b, v_b, qi, ki, wi, ga, gb


def map_query_chunks(fn, q_pos, chunk, *qs):
    T = q_pos.shape[0]
    c = math.gcd(T, chunk)
    nc = T // c
    qs_c = tuple(jnp.swapaxes(x.reshape((x.shape[0], nc, c) + x.shape[2:]), 0, 1) for x in qs)
    out = lax.map(lambda a: fn(a[0], *a[1]), (q_pos.reshape(nc, c), qs_c))
    out = jnp.swapaxes(out, 0, 1)
    return out.reshape((out.shape[0], T) + out.shape[3:])


def block_means_local(k):
    B, T = k.shape[:2]
    nb = -(-T // MOBA_BLOCK)
    kp = jnp.pad(k.astype(jnp.float32), ((0, 0), (0, nb * MOBA_BLOCK - T), (0, 0), (0, 0)))
    return (kp.reshape((B, nb, MOBA_BLOCK) + k.shape[2:]).sum(axis=2) / MOBA_BLOCK).astype(k.dtype)


def block_means_paged(cache, layer, k_new, page_table):
    ps = cache.shape[2]
    n_pages = page_table.shape[1]
    past = n_pages * ps
    B, T = k_new.shape[:2]
    nb = -(-(past + T) // MOBA_BLOCK)
    page_sum = lax.map(lambda pt: jnp.sum(cache[layer, pt].astype(jnp.float32), axis=1), page_table)
    page_blk = (jnp.arange(n_pages) * ps) // MOBA_BLOCK
    new_blk = (past + jnp.arange(T)) // MOBA_BLOCK
    sums = jnp.zeros((B, nb) + k_new.shape[2:], jnp.float32)
    sums = sums.at[:, page_blk].add(page_sum).at[:, new_blk].add(k_new.astype(jnp.float32))
    return (sums / MOBA_BLOCK).astype(k_new.dtype)


def fetch_local_heads(k, v, pos):
    H = pos.shape[2]
    g = H // k.shape[2]
    b = jnp.arange(k.shape[0])[:, None, None, None, None]
    kvh = (jnp.arange(H) // g)[None, None, :, None, None]
    p = jnp.clip(pos, 0, k.shape[1] - 1)
    return k[b, p, kvh], v[b, p, kvh]


def fetch_local(k, v, sel):
    b = jnp.arange(k.shape[0])[:, None, None]
    p = jnp.clip(sel, 0, k.shape[1] - 1)
    return k[b, p], v[b, p]


def fetch_paged(cache, layer, new, page_table, pos, kvh):
    ps = cache.shape[2]
    past = page_table.shape[1] * ps
    t_new = new.shape[1]
    b = jnp.arange(new.shape[0]).reshape((-1,) + (1,) * (pos.ndim - 1))
    pc = jnp.clip(pos, 0, past - 1)
    phys = page_table[b, pc // ps]
    pn = jnp.clip(pos - past, 0, t_new - 1)
    if kvh is None:
        old = cache[layer, phys, pc % ps]
        fresh = new[b, pn]
    else:
        old = cache[layer, phys, pc % ps, kvh]
        fresh = new[b, pn, kvh]
    is_past = (pos < past).reshape(pos.shape + (1,) * (old.ndim - pos.ndim))
    return jnp.where(is_past, old, fresh)


def moba_core(q, q_pos, block_mean, fetch):
    B, Tc, H, D = q.shape
    nb, kvh_n = block_mean.shape[1], block_mean.shape[2]
    g = H // kvh_n
    gate = jnp.einsum('btkgd,bnkd->btkgn', q.reshape(B, Tc, kvh_n, g, D), block_mean)
    gate = gate.reshape(B, Tc, H, nb).astype(jnp.float32)
    n_past = q_pos // MOBA_BLOCK
    past_ok = jnp.arange(nb)[None, :] < n_past[:, None]
    gate = jnp.where(past_ok[None, :, None, :], gate, -jnp.inf)
    ks = min(MOBA_TOPK, nb)
    _, sel = lax.top_k(gate, ks)
    own = jnp.broadcast_to(n_past[None, :, None, None], (B, Tc, H, 1)).astype(sel.dtype)
    blocks = jnp.concatenate([sel, own], axis=-1)
    pos = blocks[..., None] * MOBA_BLOCK + jnp.arange(MOBA_BLOCK)
    blk_ok = jnp.concatenate([jnp.arange(ks)[None, :] < n_past[:, None],
                              jnp.ones((Tc, 1), dtype=bool)], axis=-1)
    ok = blk_ok[None, :, None, :, None] & (pos <= q_pos[None, :, None, None, None])
    k, v = fetch(pos)
    s = jnp.einsum('bthd,bthnrd->bthnr', q, k).astype(jnp.float32) * HEAD_DIM ** -0.5
    s = jnp.where(ok, s, -jnp.inf).reshape(B, Tc, H, -1)
    p = jax.nn.softmax(s, axis=-1).astype(v.dtype).reshape(B, Tc, H, ks + 1, MOBA_BLOCK)
    return jnp.einsum('bthnr,bthnrd->bthd', p, v)


def dsa_core(q, qi, wi, q_pos, kidx_all, fetch, k_sel):
    B, Tc, H, D = q.shape
    rel = jax.nn.relu(jnp.einsum('bthd,bsd->bths', qi, kidx_all))
    score = jnp.einsum('bths,bth->bts', rel, wi).astype(jnp.float32)
    L = kidx_all.shape[1]
    causal = jnp.arange(L)[None, :] <= q_pos[:, None]
    score = jnp.where(causal[None], score, -jnp.inf)
    _, sel = lax.top_k(score, k_sel)
    ok = sel <= q_pos[None, :, None]
    k, v = fetch(sel)
    kvh_n = k.shape[3]
    g = H // kvh_n
    s = jnp.einsum('btkgd,btnkd->btkgn', q.reshape(B, Tc, kvh_n, g, D), k).astype(jnp.float32)
    s = jnp.where(ok[:, :, None, None, :], s * HEAD_DIM ** -0.5, -jnp.inf)
    p = jax.nn.softmax(s, axis=-1).astype(v.dtype)
    return jnp.einsum('btkgn,btnkd->btkgd', p, v).reshape(B, Tc, H, D)


def merge_branches(o_a, o_b, ga, gb, w_ba, w_bb, w_o):
    B, T = o_a.shape[:2]
    y_a = o_a.reshape(B, T, -1) @ w_ba
    y_b = o_b.reshape(B, T, -1) @ w_bb
    return (jax.nn.sigmoid(ga) * y_a + jax.nn.sigmoid(gb) * y_b) @ w_o


def mixers_prompt(h, pos, w_in, w_ba, w_bb, w_o):
    q_a, k_a, v_a, q_b, k_b, v_b, qi, ki, wi, ga, gb = project_mixers(h, pos, w_in)
    T = h.shape[1]
    means = block_means_local(k_a)
    o_a = map_query_chunks(lambda ps_, q: moba_core(q, ps_, means, lambda p: fetch_local_heads(k_a, v_a, p)),
                           pos, MOBA_Q_CHUNK, q_a)
    k_sel = min(DSA_TOPK, T // 4)
    o_b = map_query_chunks(lambda ps_, q, qq, ww: dsa_core(q, qq, ww, ps_, ki, lambda s: fetch_local(k_b, v_b, s), k_sel),
                           pos, DSA_Q_CHUNK, q_b, qi, wi)
    return merge_branches(o_a, o_b, ga, gb, w_ba, w_bb, w_o), (k_a, v_a, k_b, v_b, ki)


def mixers_sample(h, pos, layer, cache_a_k, cache_a_v, cache_b_k, cache_b_v, cache_b_kidx, page_table,
                  w_in, w_ba, w_bb, w_o):
    q_a, k_a, v_a, q_b, k_b, v_b, qi, ki, wi, ga, gb = project_mixers(h, pos, w_in)
    B, T = h.shape[:2]
    past = page_table.shape[1] * cache_a_k.shape[2]
    means = block_means_paged(cache_a_k, layer, k_a, page_table)
    kvh = (jnp.arange(N_HEADS_A) // (N_HEADS_A // N_KV_A))[None, None, :, None, None]
    fetch_a = lambda p: (fetch_paged(cache_a_k, layer, k_a, page_table, p, kvh),
                         fetch_paged(cache_a_v, layer, v_a, page_table, p, kvh))
    o_a = map_query_chunks(lambda ps_, q: moba_core(q, ps_, means, fetch_a), pos, MOBA_Q_CHUNK, q_a)
    kidx_all = jnp.concatenate([cache_b_kidx[layer, page_table].reshape(B, past, IDX_DIM), ki], axis=1)
    k_sel = min(DSA_TOPK, (past + T) // 4)
    fetch_b = lambda s: (fetch_paged(cache_b_k, layer, k_b, page_table, s, None),
                         fetch_paged(cache_b_v, layer, v_b, page_table, s, None))
    o_b = map_query_chunks(lambda ps_, q, qq, ww: dsa_core(q, qq, ww, ps_, kidx_all, fetch_b, k_sel),
                           pos, DSA_Q_CHUNK, q_b, qi, wi)
    return merge_branches(o_a, o_b, ga, gb, w_ba, w_bb, w_o), (k_a, v_a, k_b, v_b, ki)


def peer_chunk(h, w_q, keys1, keys2, u, v):
    n = h.shape[0]
    q = (h @ w_q).reshape(n, PEER_HEADS, 2, PEER_HALF)
    s1 = jnp.einsum('nhd,hkd->nhk', q[:, :, 0], keys1).astype(jnp.float32)
    s2 = jnp.einsum('nhd,hkd->nhk', q[:, :, 1], keys2).astype(jnp.float32)
    v1, i1 = lax.top_k(s1, PEER_TOPK)
    v2, i2 = lax.top_k(s2, PEER_TOPK)
    cand = (v1[..., :, None] + v2[..., None, :]).reshape(n, PEER_HEADS, PEER_TOPK * PEER_TOPK)
    cidx = (i1[..., :, None] * PEER_N_KEYS + i2[..., None, :]).reshape(n, PEER_HEADS, PEER_TOPK * PEER_TOPK)
    best, j = lax.top_k(cand, PEER_TOPK)
    expert = jnp.take_along_axis(cidx, j, axis=-1)
    gate = jax.nn.softmax(best, axis=-1).astype(h.dtype)
    ue = u[expert]
    ve = v[expert]
    act = jax.nn.gelu(jnp.einsum('nd,nhkd->nhk', h, ue), approximate=False)
    return jnp.einsum('nhk,nhkd->nd', gate * act, ve)


def peer(h, w_q, keys1, keys2, u, v):
    shape = h.shape
    hf = h.reshape(-1, shape[-1])
    n = hf.shape[0]
    c = min(PEER_CHUNK, n)
    nc = -(-n // c)
    hp = jnp.pad(hf, ((0, nc * c - n), (0, 0))).reshape(nc, c, shape[-1])
    out = lax.map(lambda blk: peer_chunk(blk, w_q, keys1, keys2, u, v), hp)
    return out.reshape(nc * c, shape[-1])[:n].reshape(shape)


def post_mixer(x, p, g_ffn, w_peer_q, keys1, keys2, u, v, g_ple, w_ple_gate, w_ple_proj):
    x = x + peer(rmsnorm(x, g_ffn), w_peer_q, keys1, keys2, u, v)
    return x + jax.nn.sigmoid(rmsnorm(x, g_ple) @ w_ple_gate) * (p @ w_ple_proj)


def setup_inputs(seed: int = 0) -> dict:
    key = jax.random.key(seed)
    ks = jax.random.split(key, 26)
    n_pages = PAST_LEN // PAGE_SIZE
    in_use = DEC_BATCH * n_pages
    n_pool = in_use + max(1, in_use // 4)
    f32 = jnp.float32

    def nrm(k, shape, scale=1.0):
        return jax.random.normal(k, shape, f32) * scale

    def gain(k, shape):
        return 1.0 + 0.02 * jax.random.normal(k, shape, f32)

    page_table = jax.random.permutation(ks[7], n_pool)[:in_use].reshape(DEC_BATCH, n_pages).astype(jnp.int32)
    return {
        'x_prompt': nrm(ks[0], (BATCH, SEQ, D_MODEL)),
        'x_sample': nrm(ks[1], (DEC_BATCH, DEC_SEQ, D_MODEL)),
        'cache_a_k': nrm(ks[2], (DEPTH, n_pool, PAGE_SIZE, N_KV_A, HEAD_DIM)),
        'cache_a_v': nrm(ks[3], (DEPTH, n_pool, PAGE_SIZE, N_KV_A, HEAD_DIM)),
        'cache_b_k': nrm(ks[4], (DEPTH, n_pool, PAGE_SIZE, N_KV_B, HEAD_DIM)),
        'cache_b_v': nrm(ks[5], (DEPTH, n_pool, PAGE_SIZE, N_KV_B, HEAD_DIM)),
        'cache_b_kidx': nrm(ks[6], (DEPTH, n_pool, PAGE_SIZE, IDX_DIM)),
        'page_table': page_table,
        'p_prompt': nrm(ks[8], (DEPTH, BATCH, SEQ, PLE_DIM)),
        'p_sample': nrm(ks[9], (DEPTH, DEC_BATCH, DEC_SEQ, PLE_DIM)),
        'g_attn': gain(ks[10], (DEPTH, D_MODEL)),
        'w_in': nrm(ks[11], (DEPTH, D_MODEL, IN_COLS), D_MODEL ** -0.5),
        'w_branch_a': nrm(ks[12], (DEPTH, N_HEADS_A * HEAD_DIM, D_MODEL), (N_HEADS_A * HEAD_DIM) ** -0.5),
        'w_branch_b': nrm(ks[13], (DEPTH, N_HEADS_B * HEAD_DIM, D_MODEL), (N_HEADS_B * HEAD_DIM) ** -0.5),
        'w_out': nrm(ks[14], (DEPTH, D_MODEL, D_MODEL), D_MODEL ** -0.5),
        'g_ffn': gain(ks[15], (DEPTH, D_MODEL)),
        'w_peer_q': nrm(ks[16], (DEPTH, D_MODEL, PEER_HEADS * PEER_KEY_DIM), D_MODEL ** -0.5),
        'peer_keys1': nrm(ks[17], (DEPTH, PEER_HEADS, PEER_N_KEYS, PEER_HALF), PEER_HALF ** -0.5),
        'peer_keys2': nrm(ks[18], (DEPTH, PEER_HEADS, PEER_N_KEYS, PEER_HALF), PEER_HALF ** -0.5),
        'peer_u': nrm(ks[19], (DEPTH, PEER_N_EXPERTS, D_MODEL), D_MODEL ** -0.5),
        'peer_v': nrm(ks[20], (DEPTH, PEER_N_EXPERTS, D_MODEL), PEER_HEADS ** -0.5),
        'g_ple': gain(ks[21], (DEPTH, D_MODEL)),
        'w_ple_gate': nrm(ks[22], (DEPTH, D_MODEL, D_MODEL), D_MODEL ** -0.5),
        'w_ple_proj': nrm(ks[23], (DEPTH, PLE_DIM, D_MODEL), PLE_DIM ** -0.5),
        'g_final': gain(ks[24], (D_MODEL,)),
    }


def reference(x_prompt, x_sample, cache_a_k, cache_a_v, cache_b_k, cache_b_v, cache_b_kidx, page_table,
              p_prompt, p_sample, g_attn, w_in, w_branch_a, w_branch_b, w_out, g_ffn, w_peer_q,
              peer_keys1, peer_keys2, peer_u, peer_v, g_ple, w_ple_gate, w_ple_proj, g_final):
    past = page_table.shape[1] * cache_a_k.shape[2]
    pos_p = jnp.arange(x_prompt.shape[1], dtype=jnp.int32)
    pos_s = past + jnp.arange(x_sample.shape[1], dtype=jnp.int32)
    xp, xs = x_prompt, x_sample
    rows_p = [[], [], [], [], []]
    rows_s = [[], [], [], [], []]
    for i in range(DEPTH):
        out_p, new_p = mixers_prompt(rmsnorm(xp, g_attn[i]), pos_p, w_in[i], w_branch_a[i], w_branch_b[i], w_out[i])
        xp = post_mixer(xp + out_p, p_prompt[i], g_ffn[i], w_peer_q[i], peer_keys1[i], peer_keys2[i],
                        peer_u[i], peer_v[i], g_ple[i], w_ple_gate[i], w_ple_proj[i])
        out_s, new_s = mixers_sample(rmsnorm(xs, g_attn[i]), pos_s, i, cache_a_k, cache_a_v, cache_b_k, cache_b_v,
                                     cache_b_kidx, page_table, w_in[i], w_branch_a[i], w_branch_b[i], w_out[i])
        xs = post_mixer(xs + out_s, p_sample[i], g_ffn[i], w_peer_q[i], peer_keys1[i], peer_keys2[i],
                        peer_u[i], peer_v[i], g_ple[i], w_ple_gate[i], w_ple_proj[i])
        for lst, r in zip(rows_p, new_p):
            lst.append(r)
        for lst, r in zip(rows_s, new_s):
            lst.append(r)
    y_prompt = rmsnorm(xp, g_final)
    y_sample = rmsnorm(xs, g_final)
    new_a_k_prompt = jnp.stack(rows_p[0])
    new_a_v_prompt = jnp.stack(rows_p[1])
    new_b_k_prompt = jnp.stack(rows_p[2])
    new_b_v_prompt = jnp.stack(rows_p[3])
    new_b_kidx_prompt = jnp.stack(rows_p[4])
    new_a_k_sample = jnp.stack(rows_s[0])
    new_a_v_sample = jnp.stack(rows_s[1])
    new_b_k_sample = jnp.stack(rows_s[2])
    new_b_v_sample = jnp.stack(rows_s[3])
    new_b_kidx_sample = jnp.stack(rows_s[4])
    return (y_prompt, y_sample, new_a_k_prompt, new_a_v_prompt, new_b_k_prompt, new_b_v_prompt, new_b_kidx_prompt,
            new_a_k_sample, new_a_v_sample, new_b_k_sample, new_b_v_sample, new_b_kidx_sample)
```

```python
import functools
import math

import jax
import jax.numpy as jnp
import numpy as np
from jax import lax
from jax.experimental import pallas as pl
from jax.experimental.pallas import tpu as pltpu

F32 = jnp.float32
BF16 = jnp.bfloat16
NEG_INF = float("-inf")

LANES = 128
HEAD_DIM = 128
MOBA_BLOCK = 256
MOBA_TOPK = 3
DSA_TOPK = 256
IDX_HEADS = 16
PEER_HEADS = 8
PEER_N_KEYS = 128
PEER_TOPK = 16
ROPE_THETA = 10000.0
RMS_EPS = 1e-6
VMEM_LIMIT = 56 * 1024 * 1024


def _cparams(sem):
    return pltpu.CompilerParams(dimension_semantics=sem, vmem_limit_bytes=VMEM_LIMIT)


def _dot_nt(a, b, precision=None):
    return lax.dot_general(a, b, (((1,), (1,)), ((), ())),
                           preferred_element_type=F32, precision=precision)


def _dot(a, b):
    return jnp.dot(a, b, preferred_element_type=F32)


def _rms(x, g):
    return x * lax.rsqrt(jnp.mean(x * x, axis=-1, keepdims=True) + RMS_EPS) * g


def _rmsnorm_kernel(x_ref, g_ref, o_ref):
    o_ref[...] = _rms(x_ref[...], g_ref[...]).astype(o_ref.dtype)


def rmsnorm_bf16(x, g, tm):
    M, D = x.shape
    return pl.pallas_call(
        _rmsnorm_kernel,
        grid=(M // tm,),
        in_specs=[pl.BlockSpec((tm, D), lambda i: (i, 0)),
                  pl.BlockSpec((1, D), lambda i: (0, 0))],
        out_specs=pl.BlockSpec((tm, D), lambda i: (i, 0)),
        out_shape=jax.ShapeDtypeStruct((M, D), BF16),
        compiler_params=_cparams(("parallel",)),
        name="rmsnorm",
    )(x, g.reshape(1, D))


def _proj_kernel(h_ref, w_ref, cos_ref, sin_ref, o_ref, *, rope, scale):
    acc = _dot(h_ref[...], w_ref[...])
    if scale != 1.0:
        acc = acc * scale
    if rope:
        cos = cos_ref[...]
        sin = sin_ref[...]
        for c in range(acc.shape[1] // HEAD_DIM):
            a = acc[:, c * HEAD_DIM:(c + 1) * HEAD_DIM]
            r = a * cos + pltpu.roll(a, HEAD_DIM // 2, 1) * sin
            o_ref[:, c * HEAD_DIM:(c + 1) * HEAD_DIM] = r.astype(o_ref.dtype)
    else:
        o_ref[...] = acc.astype(o_ref.dtype)


def project(h, w, cos, sin, *, rope, out_dtype, tm, tn, scale=1.0):
    M, K = h.shape
    N = w.shape[1]
    tn = min(tn, N)
    return pl.pallas_call(
        functools.partial(_proj_kernel, rope=rope, scale=scale),
        grid=(M // tm, N // tn),
        in_specs=[pl.BlockSpec((tm, K), lambda i, j: (i, 0)),
                  pl.BlockSpec((K, tn), lambda i, j: (0, j)),
                  pl.BlockSpec((tm, HEAD_DIM), lambda i, j: (i, 0)),
                  pl.BlockSpec((tm, HEAD_DIM), lambda i, j: (i, 0))],
        out_specs=pl.BlockSpec((tm, tn), lambda i, j: (i, j)),
        out_shape=jax.ShapeDtypeStruct((M, N), out_dtype),
        compiler_params=_cparams(("parallel", "arbitrary")),
        name="project_rope" if rope else "project",
    )(h, w, cos, sin)


def rope_tables(pos):
    half = HEAD_DIM // 2
    inv = ROPE_THETA ** (-jnp.arange(half, dtype=F32) / half)
    ang = pos.astype(F32)[:, None] * inv[None, :]
    cos = jnp.cos(ang)
    sin = jnp.sin(ang)
    return jnp.concatenate([cos, cos], axis=-1), jnp.concatenate([-sin, sin], axis=-1)


def _moba_prompt_kernel(q_ref, k_ref, v_ref, o_ref, kb_scr, vt_scr, mean_scr, sel_scr, *, nb):
    i = pl.program_id(2)
    blk = MOBA_BLOCK
    g = q_ref.shape[1] // HEAD_DIM
    nq = g * blk

    @pl.when(i == 0)
    def _():
        for j in range(nb):
            kj = k_ref[j * blk:(j + 1) * blk, :]
            mean_scr[j:j + 1, :] = jnp.sum(kj, axis=0, keepdims=True) / blk
            kb_scr[j] = kj.astype(BF16)
            vt_scr[j] = v_ref[j * blk:(j + 1) * blk, :].T.astype(BF16)

    q = q_ref[...]
    qs = jnp.concatenate([q[:, h * HEAD_DIM:(h + 1) * HEAD_DIM] for h in range(g)], axis=0)

    gate = _dot_nt(mean_scr[...], qs.astype(F32), precision=lax.Precision.HIGHEST)
    row = lax.broadcasted_iota(jnp.int32, (nb, nq), 0)
    valid = row < i
    gate = jnp.where(valid, gate, NEG_INF)
    rank = jnp.zeros((nb, nq), jnp.int32)
    for jp in range(nb):
        gj = gate[jp:jp + 1, :]
        beats = (gj > gate) | ((gj == gate) & (jp < row))
        rank = rank + beats.astype(jnp.int32)
    sel_scr[...] = jnp.where(valid & (rank < MOBA_TOPK), 1.0, 0.0).astype(F32)

    scale = HEAD_DIM ** -0.5
    s = _dot_nt(kb_scr[i], qs) * scale
    kpos = lax.broadcasted_iota(jnp.int32, (blk, nq), 0)
    qpos = lax.broadcasted_iota(jnp.int32, (blk, nq), 1) % blk
    s = jnp.where(kpos <= qpos, s, NEG_INF)
    m0 = jnp.max(s, axis=0, keepdims=True)
    p = jnp.exp(s - m0)
    l0 = jnp.sum(p, axis=0, keepdims=True)
    acc0 = _dot(vt_scr[i], p.astype(BF16))

    def body(j, carry):
        m, l, acc = carry
        sj = _dot_nt(kb_scr[j], qs) * scale
        selj = sel_scr[pl.ds(j, 1), :] > 0.5
        sj = jnp.where(selj, sj, NEG_INF)
        m_new = jnp.maximum(m, jnp.max(sj, axis=0, keepdims=True))
        pj = jnp.exp(sj - m_new)
        alpha = jnp.exp(m - m_new)
        l = alpha * l + jnp.sum(pj, axis=0, keepdims=True)
        acc = alpha * acc + _dot(vt_scr[j], pj.astype(BF16))
        return m_new, l, acc

    _, l, acc = lax.fori_loop(0, i, body, (m0, l0, acc0))
    out = acc / l
    for h in range(g):
        o_ref[:, h * HEAD_DIM:(h + 1) * HEAD_DIM] = out[:, h * blk:(h + 1) * blk].T.astype(o_ref.dtype)


def moba_prompt(q, k, v, B, T, H, q_col0=0):
    KVH = k.shape[1] // HEAD_DIM
    g = H // KVH
    nb = T // MOBA_BLOCK
    qb0 = q_col0 // (g * HEAD_DIM)
    return pl.pallas_call(
        functools.partial(_moba_prompt_kernel, nb=nb),
        grid=(B, KVH, nb),
        in_specs=[pl.BlockSpec((MOBA_BLOCK, g * HEAD_DIM), lambda b, k_, i: (b * nb + i, qb0 + k_)),
                  pl.BlockSpec((T, HEAD_DIM), lambda b, k_, i: (b, k_)),
                  pl.BlockSpec((T, HEAD_DIM), lambda b, k_, i: (b, k_))],
        out_specs=pl.BlockSpec((MOBA_BLOCK, g * HEAD_DIM), lambda b, k_, i: (b * nb + i, k_)),
        out_shape=jax.ShapeDtypeStruct((B * T, H * HEAD_DIM), BF16),
        scratch_shapes=[pltpu.VMEM((nb, MOBA_BLOCK, HEAD_DIM), BF16),
                        pltpu.VMEM((nb, HEAD_DIM, MOBA_BLOCK), BF16),
                        pltpu.VMEM((nb, HEAD_DIM), F32),
                        pltpu.VMEM((nb, g * MOBA_BLOCK), F32)],
        compiler_params=_cparams(("parallel", "parallel", "arbitrary")),
        name="moba_prompt",
    )(q, k, v)


def _ordered_key(x):
    u = pltpu.bitcast(x, jnp.int32)
    return u ^ ((u >> 31) & jnp.int32(0x7FFFFFFF))


def _dsa_prompt_kernel(qi_ref, wi_ref, q_ref, ki_ref, k_ref, v_ref, o_ref,
                       kib_scr, kb_scr, vt_scr, key_scr, sel_scr, *, nb, k_sel):
    i = pl.program_id(1)
    tq = qi_ref.shape[0]
    kvh_n = k_ref.shape[1] // HEAD_DIM
    h_n = q_ref.shape[1] // HEAD_DIM
    g = h_n // kvh_n
    hi_n = qi_ref.shape[1] // HEAD_DIM
    int_min = jnp.int32(-2 ** 31)

    @pl.when(i == 0)
    def _():
        for j in range(nb):
            rows = slice(j * tq, (j + 1) * tq)
            kib_scr[j] = ki_ref[rows, :].astype(BF16)
            for c in range(kvh_n):
                cols = slice(c * HEAD_DIM, (c + 1) * HEAD_DIM)
                kb_scr[c * nb + j] = k_ref[rows, cols].astype(BF16)
                vt_scr[c * nb + j] = v_ref[rows, cols].T.astype(BF16)

    wit = wi_ref[...].T * (hi_n * HEAD_DIM) ** -0.5
    qi = qi_ref[...]

    kpos = lax.broadcasted_iota(jnp.int32, (tq, tq), 0)
    qpos = lax.broadcasted_iota(jnp.int32, (tq, tq), 1)
    causal = kpos <= qpos

    def score_blk(j, _):
        kib = kib_scr[j]
        acc = jnp.zeros((tq, tq), F32)
        for h in range(hi_n):
            r = _dot_nt(kib, qi[:, h * HEAD_DIM:(h + 1) * HEAD_DIM])
            acc = acc + jnp.maximum(r, 0.0) * wit[h:h + 1, :]
        key = _ordered_key(acc)
        key = jnp.where((j < i) | causal, key, int_min)
        key_scr[j] = key
        return 0

    lax.fori_loop(0, i + 1, score_blk, 0)

    def count(pred_fn):
        def blk(j, c):
            m = pred_fn(key_scr[j], j)
            return c + jnp.sum(m.astype(jnp.int32).reshape(tq // 8, 8, tq), axis=0)
        c8 = lax.fori_loop(0, i + 1, blk, jnp.zeros((8, tq), jnp.int32))
        return jnp.sum(c8, axis=0, keepdims=True)

    def bit_step(it, t_u):
        bit = lax.shift_left(jnp.int32(1), 31 - it)
        cand_u = t_u | bit
        cand = cand_u ^ int_min
        c = count(lambda kk, j: kk >= cand)
        return jnp.where(c >= k_sel, cand_u, t_u)

    t_u = lax.fori_loop(0, 32, bit_step, jnp.zeros((1, tq), jnp.int32))
    thr = t_u ^ int_min
    need = k_sel - count(lambda kk, j: kk > thr)

    def idx_of(j):
        return lax.broadcasted_iota(jnp.int32, (tq, tq), 0) + j * tq

    n_bits = max(1, int(math.ceil(math.log2(nb * tq))))

    def tie_step(it, y):
        cand = y | lax.shift_left(jnp.int32(1), n_bits - 1 - it)
        c = count(lambda kk, j: (kk == thr) & (idx_of(j) < cand))
        return jnp.where(c < need, cand, y)

    y = lax.fori_loop(0, n_bits, tie_step, jnp.zeros((1, tq), jnp.int32))

    def sel_blk(j, _):
        kk = key_scr[j]
        s = (kk > thr) | ((kk == thr) & (idx_of(j) <= y))
        s = s & (kk > int_min)
        sel_scr[j] = jnp.where(s, 1.0, 0.0).astype(F32)
        return 0

    lax.fori_loop(0, i + 1, sel_blk, 0)

    scale = HEAD_DIM ** -0.5
    q = q_ref[...]
    for h in range(h_n):
        c = h // g
        qh = q[:, h * HEAD_DIM:(h + 1) * HEAD_DIM]

        def body(j, carry, c=c, qh=qh):
            m, l, acc = carry
            s = _dot_nt(kb_scr[c * nb + j], qh) * scale
            sel = sel_scr[j] > 0.5
            m_new = jnp.maximum(m, jnp.max(jnp.where(sel, s, -1e30), axis=0, keepdims=True))
            p = jnp.where(sel, jnp.exp(s - m_new), 0.0)
            alpha = jnp.exp(m - m_new)
            l = alpha * l + jnp.sum(p, axis=0, keepdims=True)
            acc = alpha * acc + _dot(vt_scr[c * nb + j], p.astype(BF16))
            return m_new, l, acc

        init = (jnp.full((1, tq), -1e30, F32), jnp.zeros((1, tq), F32), jnp.zeros((HEAD_DIM, tq), F32))
        _, l, acc = lax.fori_loop(0, i + 1, body, init)
        o_ref[:, h * HEAD_DIM:(h + 1) * HEAD_DIM] = (acc / l).T.astype(o_ref.dtype)


def dsa_prompt(qi, wi, q, ki, k, v, B, T, H, qi_col0=0, q_col0=0):
    tq = 256
    nb = T // tq
    k_sel = min(DSA_TOPK, T // 4)
    kvh_n = k.shape[1] // HEAD_DIM
    wqi = IDX_HEADS * HEAD_DIM
    wq = H * HEAD_DIM
    qib, qb = qi_col0 // wqi, q_col0 // wq
    return pl.pallas_call(
        functools.partial(_dsa_prompt_kernel, nb=nb, k_sel=k_sel),
        grid=(B, nb),
        in_specs=[pl.BlockSpec((tq, wqi), lambda b, i: (b * nb + i, qib)),
                  pl.BlockSpec((tq, LANES), lambda b, i: (b * nb + i, 0)),
                  pl.BlockSpec((tq, wq), lambda b, i: (b * nb + i, qb)),
                  pl.BlockSpec((T, HEAD_DIM), lambda b, i: (b, 0)),
                  pl.BlockSpec((T, k.shape[1]), lambda b, i: (b, 0)),
                  pl.BlockSpec((T, v.shape[1]), lambda b, i: (b, 0))],
        out_specs=pl.BlockSpec((tq, wq), lambda b, i: (b * nb + i, 0)),
        out_shape=jax.ShapeDtypeStruct((B * T, wq), BF16),
        scratch_shapes=[pltpu.VMEM((nb, tq, HEAD_DIM), BF16),
                        pltpu.VMEM((kvh_n * nb, tq, HEAD_DIM), BF16),
                        pltpu.VMEM((kvh_n * nb, HEAD_DIM, tq), BF16),
                        pltpu.VMEM((nb, tq, tq), jnp.int32),
                        pltpu.VMEM((nb, tq, tq), F32)],
        compiler_params=_cparams(("parallel", "arbitrary")),
        name="dsa_prompt",
    )(qi, wi, q, ki, k, v)


def _resident(shape):
    nd = len(shape)
    return pl.BlockSpec(shape, lambda *_: (0,) * nd, pipeline_mode=pl.Buffered(1))


def _merge_kernel(oa_ref, ob_ref, ga_ref, gb_ref, x_ref, wa_ref, wb_ref, wo_ref, o_ref):
    ya = _dot(oa_ref[...], wa_ref[...])
    yb = _dot(ob_ref[...], wb_ref[...])
    z = jax.nn.sigmoid(ga_ref[...].astype(F32)) * ya + jax.nn.sigmoid(gb_ref[...].astype(F32)) * yb
    o_ref[...] = x_ref[...] + _dot(z.astype(BF16), wo_ref[...])


def merge(o_a, o_b, gates, x, w_a, w_b, w_o, tm):
    M, D = x.shape
    return pl.pallas_call(
        _merge_kernel,
        grid=(M // tm,),
        in_specs=[pl.BlockSpec((tm, o_a.shape[1]), lambda i: (i, 0)),
                  pl.BlockSpec((tm, o_b.shape[1]), lambda i: (i, 0)),
                  pl.BlockSpec((tm, D), lambda i: (i, 0)),
                  pl.BlockSpec((tm, D), lambda i: (i, 1)),
                  pl.BlockSpec((tm, D), lambda i: (i, 0)),
                  _resident(w_a.shape), _resident(w_b.shape), _resident(w_o.shape)],
        out_specs=pl.BlockSpec((tm, D), lambda i: (i, 0)),
        out_shape=jax.ShapeDtypeStruct((M, D), F32),
        compiler_params=_cparams(("parallel",)),
        name="merge",
    )(o_a, o_b, gates, gates, x, w_a, w_b, w_o)


def _top_values(x, n):
    r = x.shape[0]
    iota = lax.broadcasted_iota(jnp.int32, x.shape, 0)
    vals = []
    for _ in range(n):
        m = jnp.max(x, axis=0, keepdims=True)
        vals.append(m)
        first = jnp.min(jnp.where(x == m, iota, r), axis=0, keepdims=True)
        x = jnp.where(iota == first, NEG_INF, x)
    return jnp.concatenate(vals, axis=0)


def _peer_topk_kernel(x_ref, g_ref, wq_ref, k1_ref, k2_ref, ht_ref, s1_ref, s2_ref, st_ref, q_scr):
    tm = x_ref.shape[0]
    hn = x_ref[...]
    hn = _rms(hn, g_ref[...])
    ht_ref[...] = hn.T.astype(BF16)
    q = _dot(hn.astype(BF16), wq_ref[...]).astype(BF16)
    kd = 2 * PEER_N_KEYS
    for hd in range(PEER_HEADS):
        q_scr[hd] = q[:, hd * kd:(hd + 1) * kd]

    def head(hd, _):
        qh = q_scr[hd]
        s1 = _dot_nt(k1_ref[hd], qh[:, :PEER_N_KEYS])
        s2 = _dot_nt(k2_ref[hd], qh[:, PEER_N_KEYS:])
        s1_ref[hd] = s1
        s2_ref[hd] = s2
        v1 = _top_values(s1, PEER_TOPK)
        v2 = _top_values(s2, PEER_TOPK)
        cand = jnp.concatenate([v1[a:a + 1, :] + v2 for a in range(PEER_TOPK)], axis=0)
        best = _top_values(cand, PEER_TOPK)
        z = jnp.sum(jnp.exp(best - best[0:1, :]), axis=0, keepdims=True)
        st_ref[hd] = jnp.concatenate([best[PEER_TOPK - 1:PEER_TOPK, :], v1[0:1, :], v2[0:1, :], 1.0 / z,
                                      jnp.zeros((4, tm), F32)], axis=0)
        return 0

    lax.fori_loop(0, PEER_HEADS, head, 0)


def peer_topk(x, g, w_q, keys1, keys2, tm):
    M, D = x.shape
    hp = PEER_HEADS
    return pl.pallas_call(
        _peer_topk_kernel,
        grid=(M // tm,),
        in_specs=[pl.BlockSpec((tm, D), lambda i: (i, 0)),
                  pl.BlockSpec((1, D), lambda i: (0, 0)),
                  _resident(w_q.shape), _resident(keys1.shape), _resident(keys2.shape)],
        out_specs=[pl.BlockSpec((D, tm), lambda i: (0, i)),
                   pl.BlockSpec((hp, PEER_N_KEYS, tm), lambda i: (0, 0, i)),
                   pl.BlockSpec((hp, PEER_N_KEYS, tm), lambda i: (0, 0, i)),
                   pl.BlockSpec((hp, 8, tm), lambda i: (0, 0, i))],
        out_shape=[jax.ShapeDtypeStruct((D, M), BF16),
                   jax.ShapeDtypeStruct((hp, PEER_N_KEYS, M), F32),
                   jax.ShapeDtypeStruct((hp, PEER_N_KEYS, M), F32),
                   jax.ShapeDtypeStruct((hp, 8, M), F32)],
        scratch_shapes=[pltpu.VMEM((hp, tm, 2 * PEER_N_KEYS), BF16)],
        compiler_params=_cparams(("parallel",)),
        name="peer_topk",
    )(x, g.reshape(1, D), w_q, keys1, keys2)


def _peer_dense_kernel(ht_ref, s1_ref, s2_ref, st_ref, u_ref, vt_ref, o_ref, e1_scr, e2_scr):
    e = pl.program_id(1)
    n_i1 = u_ref.shape[0] // PEER_N_KEYS

    @pl.when(e == 0)
    def _():
        o_ref[...] = jnp.zeros_like(o_ref)
        for hd in range(PEER_HEADS):
            st = st_ref[hd]
            e1_scr[hd] = jnp.exp(s1_ref[hd] - st[1:2, :]) * st[3:4, :]
            e2_scr[hd] = jnp.exp(s2_ref[hd] - st[2:3, :])

    a = _dot(u_ref[...], ht_ref[...])
    act = 0.5 * a * (1.0 + lax.erf(a * (2.0 ** -0.5)))
    zs = []
    for r in range(n_i1):
        i1 = e * n_i1 + r
        w = jnp.zeros((PEER_N_KEYS, a.shape[1]), F32)
        for hd in range(PEER_HEADS):
            t = s1_ref[hd, pl.ds(i1, 1), :] + s2_ref[hd]
            w = w + jnp.where(t >= st_ref[hd, 0:1, :], e1_scr[hd, pl.ds(i1, 1), :] * e2_scr[hd], 0.0)
        zs.append((act[r * PEER_N_KEYS:(r + 1) * PEER_N_KEYS, :] * w).astype(BF16))
    z = jnp.concatenate(zs, axis=0)
    o_ref[...] += _dot(vt_ref[...], z)


def peer_dense(ht, s1t, s2t, stats, u, vt, tm, e_blk):
    D, M = ht.shape
    E = u.shape[0]
    hp = PEER_HEADS
    return pl.pallas_call(
        _peer_dense_kernel,
        grid=(M // tm, E // e_blk),
        in_specs=[pl.BlockSpec((D, tm), lambda i, e: (0, i)),
                  pl.BlockSpec((hp, PEER_N_KEYS, tm), lambda i, e: (0, 0, i)),
                  pl.BlockSpec((hp, PEER_N_KEYS, tm), lambda i, e: (0, 0, i)),
                  pl.BlockSpec((hp, 8, tm), lambda i, e: (0, 0, i)),
                  pl.BlockSpec((e_blk, D), lambda i, e: (e, 0)),
                  pl.BlockSpec((D, e_blk), lambda i, e: (0, e))],
        out_specs=pl.BlockSpec((D, tm), lambda i, e: (0, i)),
        out_shape=jax.ShapeDtypeStruct((D, M), F32),
        scratch_shapes=[pltpu.VMEM((hp, PEER_N_KEYS, tm), F32),
                        pltpu.VMEM((hp, PEER_N_KEYS, tm), F32)],
        compiler_params=_cparams(("parallel", "arbitrary")),
        name="peer_dense",
    )(ht, s1t, s2t, stats, u, vt)


def _ple_kernel(x_ref, pt_ref, p_ref, g_ref, wg_ref, wp_ref, gf_ref, o_ref, *, final_norm):
    x2 = x_ref[...] + pt_ref[...].T
    h = _rms(x2, g_ref[...]).astype(BF16)
    gate = jax.nn.sigmoid(_dot(h, wg_ref[...]))
    x3 = x2 + gate * _dot(p_ref[...].astype(BF16), wp_ref[...])
    if final_norm:
        x3 = _rms(x3, gf_ref[...])
    o_ref[...] = x3


def ple(x, peer_t, p, g, w_gate, w_proj, g_final, tm, final_norm):
    M, D = x.shape
    return pl.pallas_call(
        functools.partial(_ple_kernel, final_norm=final_norm),
        grid=(M // tm,),
        in_specs=[pl.BlockSpec((tm, D), lambda i: (i, 0)),
                  pl.BlockSpec((D, tm), lambda i: (0, i)),
                  pl.BlockSpec((tm, p.shape[1]), lambda i: (i, 0)),
                  pl.BlockSpec((1, D), lambda i: (0, 0)),
                  _resident(w_gate.shape), _resident(w_proj.shape),
                  pl.BlockSpec((1, D), lambda i: (0, 0))],
        out_specs=pl.BlockSpec((tm, D), lambda i: (i, 0)),
        out_shape=jax.ShapeDtypeStruct((M, D), F32),
        compiler_params=_cparams(("parallel",)),
        name="ple",
    )(x, peer_t, p, g.reshape(1, D), w_gate, w_proj, g_final.reshape(1, D))


def _sample_mixers(qa, ka, va, qb, kb, vb, qi, ki, wi, cache_a_k, cache_a_v, cache_b_k, cache_b_v,
                   cache_b_kidx, layer, page_table):
    bs = qa.shape[0]
    ps = cache_a_k.shape[2]
    past = page_table.shape[1] * ps
    assert past % MOBA_BLOCK == 0
    nbp = past // MOBA_BLOCK
    scale = HEAD_DIM ** -0.5
    kvh = ka.shape[1] // HEAD_DIM

    def gathered(c):
        return c[layer][page_table].reshape((bs, past) + c.shape[3:])

    def attend(q4, k_past, v_past, mask, k_new, v_new, new_ok=True):
        s = jnp.einsum('bkgd,bskd->bkgs', q4, k_past).astype(F32) * scale
        s = jnp.where(mask, s, NEG_INF)
        s_new = jnp.einsum('bkgd,bkd->bkg', q4, k_new).astype(F32)[..., None] * scale
        s_new = jnp.where(new_ok, s_new, NEG_INF)
        p = jax.nn.softmax(jnp.concatenate([s, s_new], axis=-1), axis=-1)
        o = jnp.einsum('bkgs,bskd->bkgd', p[..., :-1], v_past) + p[..., -1:] * v_new[:, :, None, :]
        return o.reshape(bs, -1)

    cak, cav = gathered(cache_a_k), gathered(cache_a_v)
    q4 = qa.reshape(bs, kvh, -1, HEAD_DIM)
    means = cak.reshape(bs, nbp, MOBA_BLOCK, kvh, HEAD_DIM).sum(axis=2) / MOBA_BLOCK
    gate = jnp.einsum('bkgd,bnkd->bkgn', q4, means).astype(F32)
    _, sel = lax.top_k(gate, min(MOBA_TOPK, nbp))
    blk_mask = jnp.any(sel[..., None] == jnp.arange(nbp), axis=-2)
    mask = jnp.repeat(blk_mask, MOBA_BLOCK, axis=-1)
    o_a = attend(q4, cak, cav, mask, ka.reshape(bs, kvh, HEAD_DIM), va.reshape(bs, kvh, HEAD_DIM))

    kidx = gathered(cache_b_kidx)
    qi3 = qi.reshape(bs, IDX_HEADS, HEAD_DIM)
    rel = jax.nn.relu(jnp.einsum('bhd,bsd->bhs', qi3, jnp.concatenate([kidx, ki[:, None, :]], axis=1)))
    score = jnp.einsum('bhs,bh->bs', rel, wi).astype(F32)
    k_sel = min(DSA_TOPK, (past + 1) // 4)
    _, sel = lax.top_k(score, k_sel)
    keep = jnp.zeros((bs, past + 1), bool).at[jnp.arange(bs)[:, None], sel].set(True)
    q4b = qb.reshape(bs, kvh, -1, HEAD_DIM)
    s_mask = keep[:, None, None, :past]
    o_b = attend(q4b, gathered(cache_b_k), gathered(cache_b_v), s_mask,
                 kb.reshape(bs, kvh, HEAD_DIM), vb.reshape(bs, kvh, HEAD_DIM),
                 new_ok=keep[:, None, None, past:])
    return o_a, o_b


N_HEADS_A = 8
N_KV_A = 4
N_HEADS_B = 8
N_KV_B = 4
PLE_PAD_ROWS = 128


def _split_w_in(w):
    d = HEAD_DIM
    sizes = (N_HEADS_A * d, N_KV_A * d, N_KV_A * d, N_HEADS_B * d, N_KV_B * d, N_KV_B * d,
             IDX_HEADS * d, d, IDX_HEADS, w.shape[0], w.shape[0])
    offs = np.concatenate([[0], np.cumsum(sizes)])
    qa, ka, va, qb, kb, vb, qi, ki, wi, ga, gb = (w[:, offs[n]:offs[n + 1]] for n in range(len(sizes)))
    wi = jnp.pad(wi, ((0, 0), (0, LANES - IDX_HEADS)))
    c = lambda *a: jnp.concatenate(a, axis=1).astype(BF16)
    return dict(qq=c(qi, qa, qb), gates=c(ga, gb), ka=c(ka), va=c(va), kb=c(kb), vb=c(vb), ki=c(ki), wi=c(wi))


def _project_all(h, w, cos, sin, tm):
    kw = dict(tm=tm, tn=512)
    return dict(
        qq=project(h, w['qq'], cos, sin, rope=True, out_dtype=BF16, **kw),
        gates=project(h, w['gates'], cos, sin, rope=False, out_dtype=BF16, **kw),
        ka=project(h, w['ka'], cos, sin, rope=True, out_dtype=F32, **kw),
        va=project(h, w['va'], cos, sin, rope=False, out_dtype=F32, **kw),
        kb=project(h, w['kb'], cos, sin, rope=True, out_dtype=F32, **kw),
        vb=project(h, w['vb'], cos, sin, rope=False, out_dtype=F32, **kw),
        ki=project(h, w['ki'], cos, sin, rope=True, out_dtype=F32, **kw),
        wi=project(h, w['wi'], cos, sin, rope=False, out_dtype=F32, **kw),
    )


def _post_mixer(x1, p, lw, g_final, final_norm, tm_topk, tm_dense, tm_ple):
    ht, s1t, s2t, st = peer_topk(x1, lw['g_ffn'], lw['w_peer_q'], lw['keys1'], lw['keys2'], tm_topk)
    pt = peer_dense(ht, s1t, s2t, st, lw['u'], lw['vt'], tm_dense, 1024)
    return ple(x1, pt, p, lw['g_ple'], lw['w_ple_gate'], lw['w_ple_proj'], g_final, tm_ple, final_norm)


def kernel(x_prompt, x_sample, cache_a_k, cache_a_v, cache_b_k, cache_b_v, cache_b_kidx, page_table,
           p_prompt, p_sample, g_attn, w_in, w_branch_a, w_branch_b, w_out, g_ffn, w_peer_q,
           peer_keys1, peer_keys2, peer_u, peer_v, g_ple, w_ple_gate, w_ple_proj, g_final):
    B, T, D = x_prompt.shape
    bs, ts, _ = x_sample.shape
    assert ts == 1
    depth = w_in.shape[0]
    past = page_table.shape[1] * cache_a_k.shape[2]
    mp = B * T
    ms = PLE_PAD_ROWS
    qi_w = IDX_HEADS * HEAD_DIM
    qa_w = N_HEADS_A * HEAD_DIM

    cos_p, sin_p = rope_tables(jnp.arange(T, dtype=jnp.int32))
    cos_p, sin_p = jnp.tile(cos_p, (B, 1)), jnp.tile(sin_p, (B, 1))
    cos_s, sin_s = rope_tables(jnp.full((ms,), past, jnp.int32))

    xp = x_prompt.reshape(mp, D)
    xs = jnp.pad(x_sample.reshape(bs, D), ((0, ms - bs), (0, 0)))
    rows_p = [[] for _ in range(5)]
    rows_s = [[] for _ in range(5)]
    for i in range(depth):
        w = _split_w_in(w_in[i])
        lw = dict(g_ffn=g_ffn[i], w_peer_q=w_peer_q[i].astype(BF16), keys1=peer_keys1[i].astype(BF16),
                  keys2=peer_keys2[i].astype(BF16), u=peer_u[i].astype(BF16), vt=peer_v[i].T.astype(BF16),
                  g_ple=g_ple[i], w_ple_gate=w_ple_gate[i].astype(BF16), w_ple_proj=w_ple_proj[i].astype(BF16))
        w_a, w_b, w_o = w_branch_a[i].astype(BF16), w_branch_b[i].astype(BF16), w_out[i].astype(BF16)
        last = i == depth - 1

        pr = _project_all(rmsnorm_bf16(xp, g_attn[i], 512), w, cos_p, sin_p, 1024)
        o_a = moba_prompt(pr['qq'], pr['ka'], pr['va'], B, T, N_HEADS_A, q_col0=qi_w)
        o_b = dsa_prompt(pr['qq'], pr['wi'], pr['qq'], pr['ki'], pr['kb'], pr['vb'], B, T, N_HEADS_B,
                         qi_col0=0, q_col0=qi_w + qa_w)
        x1 = merge(o_a, o_b, pr['gates'], xp, w_a, w_b, w_o, 256)
        xp = _post_mixer(x1, p_prompt[i].reshape(mp, -1), lw, g_final, last, 256, 512, 256)
        for lst, name in zip(rows_p, ('ka', 'va', 'kb', 'vb', 'ki')):
            lst.append(pr[name].reshape((B, T, -1, HEAD_DIM) if name != 'ki' else (B, T, HEAD_DIM)))

        sr = _project_all(rmsnorm_bf16(xs, g_attn[i], ms), w, cos_s, sin_s, ms)
        f = lambda a: a[:bs].astype(F32)
        qq = f(sr['qq'])
        so_a, so_b = _sample_mixers(
            qq[:, qi_w:qi_w + qa_w], f(sr['ka']), f(sr['va']), qq[:, qi_w + qa_w:], f(sr['kb']), f(sr['vb']),
            qq[:, :qi_w], f(sr['ki']), f(sr['wi'])[:, :IDX_HEADS] * (IDX_HEADS * HEAD_DIM) ** -0.5,
            cache_a_k, cache_a_v, cache_b_k, cache_b_v, cache_b_kidx, i, page_table)
        padr = lambda a: jnp.pad(a, ((0, ms - bs), (0, 0))).astype(BF16)
        x1s = merge(padr(so_a), padr(so_b), sr['gates'], xs, w_a, w_b, w_o, ms)
        ps = jnp.pad(p_sample[i].reshape(bs, -1), ((0, ms - bs), (0, 0)))
        xs = _post_mixer(x1s, ps, lw, g_final, last, ms, ms, ms)
        for lst, name in zip(rows_s, ('ka', 'va', 'kb', 'vb', 'ki')):
            r = sr[name][:bs]
            lst.append(r.reshape((bs, 1, -1, HEAD_DIM) if name != 'ki' else (bs, 1, HEAD_DIM)))

    y_prompt = xp.reshape(B, T, D)
    y_sample = xs[:bs].reshape(bs, 1, D)
    return (y_prompt, y_sample) + tuple(jnp.stack(r) for r in rows_p) + tuple(jnp.stack(r) for r in rows_s)
```

```python
import functools
import math

import jax
import jax.numpy as jnp
import numpy as np
from jax import lax
from jax.experimental import pallas as pl
from jax.experimental.pallas import tpu as pltpu

F32 = jnp.float32
BF16 = jnp.bfloat16
NEG_INF = float("-inf")

LANES = 128
HEAD_DIM = 128
MOBA_BLOCK = 256
MOBA_TOPK = 3
DSA_TOPK = 256
IDX_HEADS = 16
PEER_HEADS = 8
PEER_N_KEYS = 128
PEER_TOPK = 16
ROPE_THETA = 10000.0
RMS_EPS = 1e-6
VMEM_LIMIT = 56 * 1024 * 1024


def _cparams(sem):
    return pltpu.CompilerParams(dimension_semantics=sem, vmem_limit_bytes=VMEM_LIMIT)


def _dot_nt(a, b, precision=None):
    return lax.dot_general(a, b, (((1,), (1,)), ((), ())),
                           preferred_element_type=F32, precision=precision)


def _dot(a, b):
    return jnp.dot(a, b, preferred_element_type=F32)


def _rms(x, g):
    return x * lax.rsqrt(jnp.mean(x * x, axis=-1, keepdims=True) + RMS_EPS) * g


def _rmsnorm_kernel(x_ref, g_ref, o_ref):
    o_ref[...] = _rms(x_ref[...], g_ref[...]).astype(o_ref.dtype)


def rmsnorm_bf16(x, g, tm):
    M, D = x.shape
    return pl.pallas_call(
        _rmsnorm_kernel,
        grid=(M // tm,),
        in_specs=[pl.BlockSpec((tm, D), lambda i: (i, 0)),
                  pl.BlockSpec((1, D), lambda i: (0, 0))],
        out_specs=pl.BlockSpec((tm, D), lambda i: (i, 0)),
        out_shape=jax.ShapeDtypeStruct((M, D), BF16),
        compiler_params=_cparams(("parallel",)),
        name="rmsnorm",
    )(x, g.reshape(1, D))


def _cast_kernel(x_ref, o_ref, *, transpose):
    x = x_ref[...]
    o_ref[...] = (x.T if transpose else x).astype(o_ref.dtype)


def cast_bf16(x, tr, transpose=False):
    R, C = x.shape
    return pl.pallas_call(
        functools.partial(_cast_kernel, transpose=transpose),
        grid=(R // tr,),
        in_specs=[pl.BlockSpec((tr, C), lambda i: (i, 0))],
        out_specs=pl.BlockSpec((C, tr), lambda i: (0, i)) if transpose else pl.BlockSpec((tr, C), lambda i: (i, 0)),
        out_shape=jax.ShapeDtypeStruct((C, R) if transpose else (R, C), BF16),
        compiler_params=_cparams(("parallel",)),
        name="cast_t" if transpose else "cast",
    )(x)


def _proj_kernel(h_ref, w_ref, cos_ref, sin_ref, o_ref, *, rope, scale):
    acc = _dot(h_ref[...], w_ref[...])
    if scale != 1.0:
        acc = acc * scale
    if rope:
        cos = cos_ref[...]
        sin = sin_ref[...]
        for c in range(acc.shape[1] // HEAD_DIM):
            a = acc[:, c * HEAD_DIM:(c + 1) * HEAD_DIM]
            r = a * cos + pltpu.roll(a, HEAD_DIM // 2, 1) * sin
            o_ref[:, c * HEAD_DIM:(c + 1) * HEAD_DIM] = r.astype(o_ref.dtype)
    else:
        o_ref[...] = acc.astype(o_ref.dtype)


def project(h, w, cos, sin, *, rope, out_dtype, tm, tn, scale=1.0):
    M, K = h.shape
    N = w.shape[1]
    tn = min(tn, N)
    return pl.pallas_call(
        functools.partial(_proj_kernel, rope=rope, scale=scale),
        grid=(M // tm, N // tn),
        in_specs=[pl.BlockSpec((tm, K), lambda i, j: (i, 0)),
                  pl.BlockSpec((K, tn), lambda i, j: (0, j)),
                  pl.BlockSpec((tm, HEAD_DIM), lambda i, j: (i, 0)),
                  pl.BlockSpec((tm, HEAD_DIM), lambda i, j: (i, 0))],
        out_specs=pl.BlockSpec((tm, tn), lambda i, j: (i, j)),
        out_shape=jax.ShapeDtypeStruct((M, N), out_dtype),
        compiler_params=_cparams(("parallel", "arbitrary")),
        name="project_rope" if rope else "project",
    )(h, w, cos, sin)


def rope_tables(pos):
    half = HEAD_DIM // 2
    inv = ROPE_THETA ** (-jnp.arange(half, dtype=F32) / half)
    ang = pos.astype(F32)[:, None] * inv[None, :]
    cos = jnp.cos(ang)
    sin = jnp.sin(ang)
    return jnp.concatenate([cos, cos], axis=-1), jnp.concatenate([-sin, sin], axis=-1)


def _moba_prompt_kernel(q_ref, k_ref, v_ref, o_ref, kb_scr, vt_scr, mean_scr, sel_scr, *, nb):
    i = pl.program_id(2)
    blk = MOBA_BLOCK
    g = q_ref.shape[1] // HEAD_DIM
    nq = g * blk

    @pl.when(i == 0)
    def _():
        for j in range(nb):
            kj = k_ref[j * blk:(j + 1) * blk, :]
            mean_scr[j:j + 1, :] = jnp.sum(kj, axis=0, keepdims=True) / blk
            kb_scr[j] = kj.astype(BF16)
            vt_scr[j] = v_ref[j * blk:(j + 1) * blk, :].T.astype(BF16)

    q = q_ref[...]
    qs = jnp.concatenate([q[:, h * HEAD_DIM:(h + 1) * HEAD_DIM] for h in range(g)], axis=0)

    gate = _dot_nt(mean_scr[...], qs.astype(F32), precision=lax.Precision.HIGHEST)
    row = lax.broadcasted_iota(jnp.int32, (nb, nq), 0)
    valid = row < i
    gate = jnp.where(valid, gate, NEG_INF)
    rank = jnp.zeros((nb, nq), jnp.int32)
    for jp in range(nb):
        gj = gate[jp:jp + 1, :]
        beats = (gj > gate) | ((gj == gate) & (jp < row))
        rank = rank + beats.astype(jnp.int32)
    sel_scr[...] = jnp.where(valid & (rank < MOBA_TOPK), 1.0, 0.0).astype(F32)

    scale = HEAD_DIM ** -0.5
    s = _dot_nt(kb_scr[i], qs) * scale
    kpos = lax.broadcasted_iota(jnp.int32, (blk, nq), 0)
    qpos = lax.broadcasted_iota(jnp.int32, (blk, nq), 1) % blk
    s = jnp.where(kpos <= qpos, s, NEG_INF)
    m0 = jnp.max(s, axis=0, keepdims=True)
    p = jnp.exp(s - m0)
    l0 = jnp.sum(p, axis=0, keepdims=True)
    acc0 = _dot(vt_scr[i], p.astype(BF16))

    def body(j, carry):
        m, l, acc = carry
        sj = _dot_nt(kb_scr[j], qs) * scale
        selj = sel_scr[pl.ds(j, 1), :] > 0.5
        sj = jnp.where(selj, sj, NEG_INF)
        m_new = jnp.maximum(m, jnp.max(sj, axis=0, keepdims=True))
        pj = jnp.exp(sj - m_new)
        alpha = jnp.exp(m - m_new)
        l = alpha * l + jnp.sum(pj, axis=0, keepdims=True)
        acc = alpha * acc + _dot(vt_scr[j], pj.astype(BF16))
        return m_new, l, acc

    _, l, acc = lax.fori_loop(0, i, body, (m0, l0, acc0))
    out = acc / l
    for h in range(g):
        o_ref[:, h * HEAD_DIM:(h + 1) * HEAD_DIM] = out[:, h * blk:(h + 1) * blk].T.astype(o_ref.dtype)


def moba_prompt(q, k, v, B, T, H, q_col0=0):
    KVH = k.shape[1] // HEAD_DIM
    g = H // KVH
    nb = T // MOBA_BLOCK
    qb0 = q_col0 // (g * HEAD_DIM)
    return pl.pallas_call(
        functools.partial(_moba_prompt_kernel, nb=nb),
        grid=(B, KVH, nb),
        in_specs=[pl.BlockSpec((MOBA_BLOCK, g * HEAD_DIM), lambda b, k_, i: (b * nb + i, qb0 + k_)),
                  pl.BlockSpec((T, HEAD_DIM), lambda b, k_, i: (b, k_)),
                  pl.BlockSpec((T, HEAD_DIM), lambda b, k_, i: (b, k_))],
        out_specs=pl.BlockSpec((MOBA_BLOCK, g * HEAD_DIM), lambda b, k_, i: (b * nb + i, k_)),
        out_shape=jax.ShapeDtypeStruct((B * T, H * HEAD_DIM), BF16),
        scratch_shapes=[pltpu.VMEM((nb, MOBA_BLOCK, HEAD_DIM), BF16),
                        pltpu.VMEM((nb, HEAD_DIM, MOBA_BLOCK), BF16),
                        pltpu.VMEM((nb, HEAD_DIM), F32),
                        pltpu.VMEM((nb, g * MOBA_BLOCK), F32)],
        compiler_params=_cparams(("parallel", "parallel", "arbitrary")),
        name="moba_prompt",
    )(q, k, v)


def _ordered_key(x):
    u = pltpu.bitcast(x, jnp.int32)
    return u ^ ((u >> 31) & jnp.int32(0x7FFFFFFF))


def _dsa_prompt_kernel(qi_ref, wi_ref, q_ref, ki_ref, k_ref, v_ref, o_ref,
                       kib_scr, kb_scr, vt_scr, key_scr, sel_scr, *, nb, k_sel):
    i = pl.program_id(1)
    tq = qi_ref.shape[0]
    kvh_n = k_ref.shape[1] // HEAD_DIM
    h_n = q_ref.shape[1] // HEAD_DIM
    g = h_n // kvh_n
    hi_n = qi_ref.shape[1] // HEAD_DIM
    int_min = jnp.int32(-2 ** 31)

    @pl.when(i == 0)
    def _():
        for j in range(nb):
            rows = slice(j * tq, (j + 1) * tq)
            kib_scr[j] = ki_ref[rows, :].astype(BF16)
            for c in range(kvh_n):
                cols = slice(c * HEAD_DIM, (c + 1) * HEAD_DIM)
                kb_scr[c * nb + j] = k_ref[rows, cols].astype(BF16)
                vt_scr[c * nb + j] = v_ref[rows, cols].T.astype(BF16)

    wit = wi_ref[...].T * (hi_n * HEAD_DIM) ** -0.5
    qi = qi_ref[...]

    kpos = lax.broadcasted_iota(jnp.int32, (tq, tq), 0)
    qpos = lax.broadcasted_iota(jnp.int32, (tq, tq), 1)
    causal = kpos <= qpos

    def score_blk(j, _):
        kib = kib_scr[j]
        acc = jnp.zeros((tq, tq), F32)
        for h in range(hi_n):
            r = _dot_nt(kib, qi[:, h * HEAD_DIM:(h + 1) * HEAD_DIM])
            acc = acc + jnp.maximum(r, 0.0) * wit[h:h + 1, :]
        key = _ordered_key(acc)
        key = jnp.where((j < i) | causal, key, int_min)
        key_scr[j] = key
        return 0

    lax.fori_loop(0, i + 1, score_blk, 0)

    def count(pred_fn):
        def blk(j, c):
            m = pred_fn(key_scr[j], j)
            return c + jnp.sum(m.astype(jnp.int32).reshape(tq // 8, 8, tq), axis=0)
        c8 = lax.fori_loop(0, i + 1, blk, jnp.zeros((8, tq), jnp.int32))
        return jnp.sum(c8, axis=0, keepdims=True)

    def bit_step(it, t_u):
        bit = lax.shift_left(jnp.int32(1), 31 - it)
        cand_u = t_u | bit
        cand = cand_u ^ int_min
        c = count(lambda kk, j: kk >= cand)
        return jnp.where(c >= k_sel, cand_u, t_u)

    t_u = lax.fori_loop(0, 32, bit_step, jnp.zeros((1, tq), jnp.int32))
    thr = t_u ^ int_min
    need = k_sel - count(lambda kk, j: kk > thr)

    def idx_of(j):
        return lax.broadcasted_iota(jnp.int32, (tq, tq), 0) + j * tq

    n_bits = max(1, int(math.ceil(math.log2(nb * tq))))

    def tie_step(it, y):
        cand = y | lax.shift_left(jnp.int32(1), n_bits - 1 - it)
        c = count(lambda kk, j: (kk == thr) & (idx_of(j) < cand))
        return jnp.where(c < need, cand, y)

    y = lax.fori_loop(0, n_bits, tie_step, jnp.zeros((1, tq), jnp.int32))

    def sel_blk(j, _):
        kk = key_scr[j]
        s = (kk > thr) | ((kk == thr) & (idx_of(j) <= y))
        s = s & (kk > int_min)
        sel_scr[j] = jnp.where(s, 1.0, 0.0).astype(F32)
        return 0

    lax.fori_loop(0, i + 1, sel_blk, 0)

    scale = HEAD_DIM ** -0.5
    q = q_ref[...]
    for h in range(h_n):
        c = h // g
        qh = q[:, h * HEAD_DIM:(h + 1) * HEAD_DIM]

        def body(j, carry, c=c, qh=qh):
            m, l, acc = carry
            s = _dot_nt(kb_scr[c * nb + j], qh) * scale
            sel = sel_scr[j] > 0.5
            m_new = jnp.maximum(m, jnp.max(jnp.where(sel, s, -1e30), axis=0, keepdims=True))
            p = jnp.where(sel, jnp.exp(s - m_new), 0.0)
            alpha = jnp.exp(m - m_new)
            l = alpha * l + jnp.sum(p, axis=0, keepdims=True)
            acc = alpha * acc + _dot(vt_scr[c * nb + j], p.astype(BF16))
            return m_new, l, acc

        init = (jnp.full((1, tq), -1e30, F32), jnp.zeros((1, tq), F32), jnp.zeros((HEAD_DIM, tq), F32))
        _, l, acc = lax.fori_loop(0, i + 1, body, init)
        o_ref[:, h * HEAD_DIM:(h + 1) * HEAD_DIM] = (acc / l).T.astype(o_ref.dtype)


def dsa_prompt(qi, wi, q, ki, k, v, B, T, H, qi_col0=0, q_col0=0):
    tq = 256
    nb = T // tq
    k_sel = min(DSA_TOPK, T // 4)
    kvh_n = k.shape[1] // HEAD_DIM
    wqi = IDX_HEADS * HEAD_DIM
    wq = H * HEAD_DIM
    qib, qb = qi_col0 // wqi, q_col0 // wq
    return pl.pallas_call(
        functools.partial(_dsa_prompt_kernel, nb=nb, k_sel=k_sel),
        grid=(B, nb),
        in_specs=[pl.BlockSpec((tq, wqi), lambda b, i: (b * nb + i, qib)),
                  pl.BlockSpec((tq, LANES), lambda b, i: (b * nb + i, 0)),
                  pl.BlockSpec((tq, wq), lambda b, i: (b * nb + i, qb)),
                  pl.BlockSpec((T, HEAD_DIM), lambda b, i: (b, 0)),
                  pl.BlockSpec((T, k.shape[1]), lambda b, i: (b, 0)),
                  pl.BlockSpec((T, v.shape[1]), lambda b, i: (b, 0))],
        out_specs=pl.BlockSpec((tq, wq), lambda b, i: (b * nb + i, 0)),
        out_shape=jax.ShapeDtypeStruct((B * T, wq), BF16),
        scratch_shapes=[pltpu.VMEM((nb, tq, HEAD_DIM), BF16),
                        pltpu.VMEM((kvh_n * nb, tq, HEAD_DIM), BF16),
                        pltpu.VMEM((kvh_n * nb, HEAD_DIM, tq), BF16),
                        pltpu.VMEM((nb, tq, tq), jnp.int32),
                        pltpu.VMEM((nb, tq, tq), F32)],
        compiler_params=_cparams(("parallel", "arbitrary")),
        name="dsa_prompt",
    )(qi, wi, q, ki, k, v)


def _resident(shape):
    nd = len(shape)
    return pl.BlockSpec(shape, lambda *_: (0,) * nd, pipeline_mode=pl.Buffered(1))


def _merge_kernel(oa_ref, ob_ref, ga_ref, gb_ref, x_ref, wa_ref, wb_ref, wo_ref, o_ref):
    ya = _dot(oa_ref[...], wa_ref[...])
    yb = _dot(ob_ref[...], wb_ref[...])
    z = jax.nn.sigmoid(ga_ref[...].astype(F32)) * ya + jax.nn.sigmoid(gb_ref[...].astype(F32)) * yb
    o_ref[...] = x_ref[...] + _dot(z.astype(BF16), wo_ref[...])


def merge(o_a, o_b, gates, x, w_a, w_b, w_o, tm):
    M, D = x.shape
    return pl.pallas_call(
        _merge_kernel,
        grid=(M // tm,),
        in_specs=[pl.BlockSpec((tm, o_a.shape[1]), lambda i: (i, 0)),
                  pl.BlockSpec((tm, o_b.shape[1]), lambda i: (i, 0)),
                  pl.BlockSpec((tm, D), lambda i: (i, 0)),
                  pl.BlockSpec((tm, D), lambda i: (i, 1)),
                  pl.BlockSpec((tm, D), lambda i: (i, 0)),
                  _resident(w_a.shape), _resident(w_b.shape), _resident(w_o.shape)],
        out_specs=pl.BlockSpec((tm, D), lambda i: (i, 0)),
        out_shape=jax.ShapeDtypeStruct((M, D), F32),
        compiler_params=_cparams(("parallel",)),
        name="merge",
    )(o_a, o_b, gates, gates, x, w_a, w_b, w_o)


def _top_values(x, n, order=None):
    if order is None:
        order = lax.broadcasted_iota(jnp.int32, x.shape, 0)
    big = jnp.int32(2 ** 30)
    vals = []
    for _ in range(n):
        m = jnp.max(x, axis=0, keepdims=True)
        vals.append(m)
        first = jnp.min(jnp.where(x == m, order, big), axis=0, keepdims=True)
        x = jnp.where(order == first, NEG_INF, x)
    return jnp.concatenate(vals, axis=0)


def _pair_candidates(v1, v2):
    k = PEER_TOPK
    tm = v1.shape[1]
    sub = lax.broadcasted_iota(jnp.int32, (8, tm), 0)
    vals = [v1[0:1, :] + v2]
    order = [lax.broadcasted_iota(jnp.int32, (k, tm), 0)]
    for a in range(1, 8):
        vals.append(jnp.where(sub < k // (a + 1), v1[a:a + 1, :] + v2[0:8, :], NEG_INF))
        order.append(sub + a * k)
    vals.append(v1[8:16, :] + v2[0:1, :])
    order.append((sub + 8) * k)
    return jnp.concatenate(vals, axis=0), jnp.concatenate(order, axis=0)


def _peer_topk_kernel(x_ref, g_ref, wq_ref, k1_ref, k2_ref, ht_ref, s1_ref, s2_ref, st_ref, q_scr):
    tm = x_ref.shape[0]
    hn = x_ref[...]
    hn = _rms(hn, g_ref[...])
    ht_ref[...] = hn.T.astype(BF16)
    q = _dot(hn.astype(BF16), wq_ref[...]).astype(BF16)
    kd = 2 * PEER_N_KEYS
    for hd in range(PEER_HEADS):
        q_scr[hd] = q[:, hd * kd:(hd + 1) * kd]

    def head(hd, _):
        qh = q_scr[hd]
        s1 = _dot_nt(k1_ref[hd], qh[:, :PEER_N_KEYS])
        s2 = _dot_nt(k2_ref[hd], qh[:, PEER_N_KEYS:])
        s1_ref[hd] = s1
        s2_ref[hd] = s2
        for tc in range(tm // LANES):
            cols = slice(tc * LANES, (tc + 1) * LANES)
            v1 = _top_values(s1[:, cols], PEER_TOPK)
            v2 = _top_values(s2[:, cols], PEER_TOPK)
            cand, order = _pair_candidates(v1, v2)
            best = _top_values(cand, PEER_TOPK, order)
            z = jnp.sum(jnp.exp(best - best[0:1, :]), axis=0, keepdims=True)
            st_ref[hd, :, cols] = jnp.concatenate(
                [best[PEER_TOPK - 1:PEER_TOPK, :], v1[0:1, :], v2[0:1, :], 1.0 / z, jnp.zeros((4, LANES), F32)], axis=0)
        return 0

    lax.fori_loop(0, PEER_HEADS, head, 0)


def peer_topk(x, g, w_q, keys1, keys2, tm):
    M, D = x.shape
    hp = PEER_HEADS
    return pl.pallas_call(
        _peer_topk_kernel,
        grid=(M // tm,),
        in_specs=[pl.BlockSpec((tm, D), lambda i: (i, 0)),
                  pl.BlockSpec((1, D), lambda i: (0, 0)),
                  _resident(w_q.shape), _resident(keys1.shape), _resident(keys2.shape)],
        out_specs=[pl.BlockSpec((D, tm), lambda i: (0, i)),
                   pl.BlockSpec((hp, PEER_N_KEYS, tm), lambda i: (0, 0, i)),
                   pl.BlockSpec((hp, PEER_N_KEYS, tm), lambda i: (0, 0, i)),
                   pl.BlockSpec((hp, 8, tm), lambda i: (0, 0, i))],
        out_shape=[jax.ShapeDtypeStruct((D, M), BF16),
                   jax.ShapeDtypeStruct((hp, PEER_N_KEYS, M), F32),
                   jax.ShapeDtypeStruct((hp, PEER_N_KEYS, M), F32),
                   jax.ShapeDtypeStruct((hp, 8, M), F32)],
        scratch_shapes=[pltpu.VMEM((hp, tm, 2 * PEER_N_KEYS), BF16)],
        compiler_params=_cparams(("parallel",)),
        name="peer_topk",
    )(x, g.reshape(1, D), w_q, keys1, keys2)


def _peer_dense_kernel(ht_ref, s1_ref, s2_ref, st_ref, u_ref, vt_ref, o_ref, e1_scr, e2_scr):
    e = pl.program_id(1)
    n_i1 = u_ref.shape[0] // PEER_N_KEYS

    @pl.when(e == 0)
    def _():
        o_ref[...] = jnp.zeros_like(o_ref)
        for hd in range(PEER_HEADS):
            st = st_ref[hd]
            e1_scr[hd] = jnp.exp(s1_ref[hd] - st[1:2, :]) * st[3:4, :]
            e2_scr[hd] = jnp.exp(s2_ref[hd] - st[2:3, :])

    a = _dot(u_ref[...], ht_ref[...])
    act = 0.5 * a * (1.0 + lax.erf(a * (2.0 ** -0.5)))
    zs = []
    for r in range(n_i1):
        i1 = e * n_i1 + r
        w = jnp.zeros((PEER_N_KEYS, a.shape[1]), F32)
        for hd in range(PEER_HEADS):
            t = s1_ref[hd, pl.ds(i1, 1), :] + s2_ref[hd]
            w = w + jnp.where(t >= st_ref[hd, 0:1, :], e1_scr[hd, pl.ds(i1, 1), :] * e2_scr[hd], 0.0)
        zs.append((act[r * PEER_N_KEYS:(r + 1) * PEER_N_KEYS, :] * w).astype(BF16))
    z = jnp.concatenate(zs, axis=0)
    o_ref[...] += _dot(vt_ref[...], z)


def peer_dense(ht, s1t, s2t, stats, u, vt, tm, e_blk):
    D, M = ht.shape
    E = u.shape[0]
    hp = PEER_HEADS
    return pl.pallas_call(
        _peer_dense_kernel,
        grid=(M // tm, E // e_blk),
        in_specs=[pl.BlockSpec((D, tm), lambda i, e: (0, i)),
                  pl.BlockSpec((hp, PEER_N_KEYS, tm), lambda i, e: (0, 0, i)),
                  pl.BlockSpec((hp, PEER_N_KEYS, tm), lambda i, e: (0, 0, i)),
                  pl.BlockSpec((hp, 8, tm), lambda i, e: (0, 0, i)),
                  pl.BlockSpec((e_blk, D), lambda i, e: (e, 0)),
                  pl.BlockSpec((D, e_blk), lambda i, e: (0, e))],
        out_specs=pl.BlockSpec((D, tm), lambda i, e: (0, i)),
        out_shape=jax.ShapeDtypeStruct((D, M), F32),
        scratch_shapes=[pltpu.VMEM((hp, PEER_N_KEYS, tm), F32),
                        pltpu.VMEM((hp, PEER_N_KEYS, tm), F32)],
        compiler_params=_cparams(("parallel", "arbitrary")),
        name="peer_dense",
    )(ht, s1t, s2t, stats, u, vt)


def _ple_kernel(x_ref, pt_ref, p_ref, g_ref, wg_ref, wp_ref, gf_ref, o_ref, *, final_norm):
    x2 = x_ref[...] + pt_ref[...].T
    h = _rms(x2, g_ref[...]).astype(BF16)
    gate = jax.nn.sigmoid(_dot(h, wg_ref[...]))
    x3 = x2 + gate * _dot(p_ref[...].astype(BF16), wp_ref[...])
    if final_norm:
        x3 = _rms(x3, gf_ref[...])
    o_ref[...] = x3


def ple(x, peer_t, p, g, w_gate, w_proj, g_final, tm, final_norm):
    M, D = x.shape
    return pl.pallas_call(
        functools.partial(_ple_kernel, final_norm=final_norm),
        grid=(M // tm,),
        in_specs=[pl.BlockSpec((tm, D), lambda i: (i, 0)),
                  pl.BlockSpec((D, tm), lambda i: (0, i)),
                  pl.BlockSpec((tm, p.shape[1]), lambda i: (i, 0)),
                  pl.BlockSpec((1, D), lambda i: (0, 0)),
                  _resident(w_gate.shape), _resident(w_proj.shape),
                  pl.BlockSpec((1, D), lambda i: (0, 0))],
        out_specs=pl.BlockSpec((tm, D), lambda i: (i, 0)),
        out_shape=jax.ShapeDtypeStruct((M, D), F32),
        compiler_params=_cparams(("parallel",)),
        name="ple",
    )(x, peer_t, p, g.reshape(1, D), w_gate, w_proj, g_final.reshape(1, D))


SAMPLE_PPS = 8
NBP_PAD = 128


def _sample_scan_kernel(pt_ref, qi_ref, wi_ref, kin_ref, qa_ref, cak_ref, cki_ref, isc_ref, inew_ref, ids_ref,
                        ak_buf, ki_buf, sums_scr, sem, *, layer, pps, ps, nbp, n_chunks):
    b = pl.program_id(0)
    kvh_n = sums_scr.shape[0]
    ppb = MOBA_BLOCK // ps
    qi = qi_ref[...]
    w = wi_ref[...]

    def copies(ch, slot):
        out = []
        for r in range(pps):
            phys = pt_ref[b, ch * pps + r]
            out.append(pltpu.make_async_copy(cak_ref.at[layer, phys], ak_buf.at[slot, r], sem.at[0, slot]))
            out.append(pltpu.make_async_copy(cki_ref.at[layer, phys], ki_buf.at[slot, r], sem.at[1, slot]))
        return out

    for c in copies(0, 0):
        c.start()
    sums_scr[...] = jnp.zeros_like(sums_scr)
    kin = kin_ref[...].astype(BF16).astype(F32)
    r_new = jnp.sum(qi.astype(F32) * kin, axis=1, keepdims=True)
    inew_ref[...] = jnp.broadcast_to(jnp.sum(jnp.maximum(r_new, 0.0) * w, axis=0, keepdims=True), inew_ref.shape)

    def chunk(ch, _):
        slot = lax.rem(ch, 2)

        @pl.when(ch + 1 < n_chunks)
        def _():
            for c in copies(ch + 1, 1 - slot):
                c.start()

        for c in copies(ch, slot):
            c.wait()
        for r in range(pps):
            x = ak_buf[slot, r]
            n = ps
            while n > 1:
                n //= 2
                x = x[:n] + x[n:]
            blk = lax.div(ch * pps + r, ppb)
            for c in range(kvh_n):
                sums_scr[c, pl.ds(blk, 1), :] += x[0, c:c + 1, :]
            sc = _dot_nt(qi, ki_buf[slot, r].astype(BF16))
            isc_ref[pl.ds(ch * pps + r, 1), :] = jnp.sum(jnp.maximum(sc, 0.0) * w, axis=0, keepdims=True)
        return 0

    lax.fori_loop(0, n_chunks, chunk, 0)

    qa = qa_ref[...].astype(F32)
    h_n = qa.shape[0]
    g = h_n // kvh_n
    row = lax.broadcasted_iota(jnp.int32, (h_n, NBP_PAD), 0)
    lane = lax.broadcasted_iota(jnp.int32, (h_n, NBP_PAD), 1)
    gate = jnp.full((h_n, NBP_PAD), NEG_INF, F32)
    for c in range(kvh_n):
        gc = _dot_nt(qa, sums_scr[c] / MOBA_BLOCK, precision=lax.Precision.HIGHEST)
        gate = jnp.where((row >= c * g) & (row < (c + 1) * g), gc, gate)
    gate = jnp.where(lane < nbp, gate, NEG_INF)
    ids = jnp.zeros((h_n, NBP_PAD), jnp.int32)
    for r in range(MOBA_TOPK):
        m = jnp.max(gate, axis=1, keepdims=True)
        first = jnp.min(jnp.where(gate == m, lane, NBP_PAD), axis=1, keepdims=True)
        ids = jnp.where(lane == r, first, ids)
        gate = jnp.where(lane == first, NEG_INF, gate)
    ids_ref[...] = ids


def sample_scan(page_table, cache_a_k, cache_b_kidx, layer, qi, wi, ki_new, qa):
    bs, n_pages = page_table.shape
    ps, kvh_n = cache_a_k.shape[2], cache_a_k.shape[3]
    pps = SAMPLE_PPS
    nbp = n_pages * ps // MOBA_BLOCK
    h_n, hi_n = qa.shape[1], qi.shape[1]
    per_seq = lambda shape: pl.BlockSpec((None,) + shape, lambda b, pt: (b,) + (0,) * len(shape))
    hbm = pl.BlockSpec(memory_space=pl.ANY)
    grid_spec = pltpu.PrefetchScalarGridSpec(
        num_scalar_prefetch=1,
        grid=(bs,),
        in_specs=[per_seq((hi_n, HEAD_DIM)), per_seq((hi_n, 1)), per_seq((1, HEAD_DIM)), per_seq((h_n, HEAD_DIM)),
                  hbm, hbm],
        out_specs=[per_seq((n_pages, ps)), per_seq((1, LANES)), per_seq((h_n, NBP_PAD))],
        scratch_shapes=[pltpu.VMEM((2, pps, ps, kvh_n, HEAD_DIM), F32),
                        pltpu.VMEM((2, pps, ps, HEAD_DIM), F32),
                        pltpu.VMEM((kvh_n, NBP_PAD, HEAD_DIM), F32),
                        pltpu.SemaphoreType.DMA((2, 2))],
    )
    return pl.pallas_call(
        functools.partial(_sample_scan_kernel, layer=layer, pps=pps, ps=ps, nbp=nbp, n_chunks=n_pages // pps),
        grid_spec=grid_spec,
        out_shape=[jax.ShapeDtypeStruct((bs, n_pages, ps), F32),
                   jax.ShapeDtypeStruct((bs, 1, LANES), F32),
                   jax.ShapeDtypeStruct((bs, h_n, NBP_PAD), jnp.int32)],
        compiler_params=_cparams(("arbitrary",)),
        name="sample_scan",
    )(page_table, qi, wi, ki_new, qa, cache_a_k, cache_b_kidx)


def _sample_select_kernel(isc_ref, inew_ref, pos_ref, sel_scr, new_scr, rank_scr, *, k_sel, past):
    bs, n_pages, ps = isc_ref.shape
    int_min = jnp.int32(-2 ** 31)
    keys = _ordered_key(isc_ref[...])
    knew = _ordered_key(inew_ref[...][:, :, 0:1])
    idx = (lax.broadcasted_iota(jnp.int32, keys.shape, 1) * ps
           + lax.broadcasted_iota(jnp.int32, keys.shape, 2))

    def count(m_cache, m_new):
        c = jnp.sum(m_cache.astype(jnp.int32), axis=1, keepdims=True)
        return jnp.sum(c, axis=2, keepdims=True) + m_new.astype(jnp.int32)

    def bit_step(it, t_u):
        cand_u = t_u | lax.shift_left(jnp.int32(1), 31 - it)
        cand = cand_u ^ int_min
        return jnp.where(count(keys >= cand, knew >= cand) >= k_sel, cand_u, t_u)

    thr = lax.fori_loop(0, 32, bit_step, jnp.zeros((bs, 1, 1), jnp.int32)) ^ int_min
    need = k_sel - count(keys > thr, knew > thr)
    n_bits = int(math.ceil(math.log2(past + 1)))

    def tie_step(it, y):
        cand = y | lax.shift_left(jnp.int32(1), n_bits - 1 - it)
        c = count((keys == thr) & (idx < cand), (knew == thr) & (past < cand))
        return jnp.where(c < need, cand, y)

    y = lax.fori_loop(0, n_bits, tie_step, jnp.zeros((bs, 1, 1), jnp.int32))
    sel_scr[...] = jnp.where((keys > thr) | ((keys == thr) & (idx <= y)), 1.0, 0.0).astype(F32)
    new_sel = (knew > thr) | ((knew == thr) & (past <= y))
    new_scr[...] = jnp.broadcast_to(jnp.where(new_sel, 1.0, 0.0).astype(F32), new_scr.shape)

    upper = (lax.broadcasted_iota(jnp.int32, (ps, ps), 0) < lax.broadcasted_iota(jnp.int32, (ps, ps), 1))
    lower = (lax.broadcasted_iota(jnp.int32, (n_pages, n_pages), 1)
             < lax.broadcasted_iota(jnp.int32, (n_pages, n_pages), 0))
    upper = jnp.where(upper, 1.0, 0.0).astype(BF16)
    lower = jnp.where(lower, 1.0, 0.0).astype(BF16)
    r_iota = lax.broadcasted_iota(jnp.int32, (k_sel, ps), 0).astype(F32)
    slot = lax.broadcasted_iota(jnp.int32, (k_sel, ps), 1).astype(F32)
    lane = lax.broadcasted_iota(jnp.int32, (1, k_sel), 1)

    def per_seq(b, _):
        sb = sel_scr[b]
        prefix = _dot(sb.astype(BF16), upper)
        cnt = jnp.sum(sb, axis=1, keepdims=True)
        offs = _dot(lower, jnp.broadcast_to(cnt, sb.shape).astype(BF16))
        rank_scr[...] = offs + prefix

        def per_page(p, acc):
            hit = (r_iota == rank_scr[pl.ds(p, 1), :]) & (sel_scr[b, pl.ds(p, 1), :] > 0.5)
            return acc + jnp.where(hit, lax.convert_element_type(p * ps, F32) + slot, 0.0)

        acc = lax.fori_loop(0, n_pages, per_page, jnp.zeros((k_sel, ps), F32))
        tot = jnp.sum(acc.T, axis=0, keepdims=True)
        tot = tot + jnp.where(lane == k_sel - 1, new_scr[b][:, 0:1] * past, 0.0)
        pos_ref[b] = tot.astype(jnp.int32)
        return 0

    lax.fori_loop(0, bs, per_seq, 0)


def sample_select(iscore, inew, k_sel, past):
    bs, n_pages, ps = iscore.shape
    return pl.pallas_call(
        functools.partial(_sample_select_kernel, k_sel=k_sel, past=past),
        out_shape=jax.ShapeDtypeStruct((bs, 1, k_sel), jnp.int32),
        scratch_shapes=[pltpu.VMEM((bs, n_pages, ps), F32),
                        pltpu.VMEM((bs, 1, LANES), F32),
                        pltpu.VMEM((n_pages, ps), F32)],
        compiler_params=pltpu.CompilerParams(vmem_limit_bytes=VMEM_LIMIT),
        name="sample_select",
    )(iscore, inew)


def _sample_attend_kernel(pt_ref, ids_ref, pos_ref, qa_ref, kan_ref, van_ref, qb_ref, kbn_ref, vbn_ref,
                          cak_ref, cav_ref, cbk_ref, cbv_ref, oa_ref, ob_ref,
                          ka_buf, va_buf, kb_buf, vb_buf, sem, *, layer, ps, past, k_sel):
    b = pl.program_id(0)
    h_n = qa_ref.shape[0]
    kvh_n = kan_ref.shape[0]
    g = h_n // kvh_n
    ppb = MOBA_BLOCK // ps
    scale = HEAD_DIM ** -0.5

    def moba_copies(h, r, half):
        phys = pt_ref[b, ids_ref[b, h * MOBA_TOPK + r] * ppb + half]
        rows = pl.ds((r * ppb + half) * ps, ps)
        return (pltpu.make_async_copy(cak_ref.at[layer, phys, :, h // g, :], ka_buf.at[h, rows, :], sem.at[0]),
                pltpu.make_async_copy(cav_ref.at[layer, phys, :, h // g, :], va_buf.at[h, rows, :], sem.at[1]))

    def dsa_copies(j):
        s = jnp.minimum(pos_ref[b, j], past - 1)
        phys = pt_ref[b, lax.div(s, ps)]
        row = lax.rem(s, ps)
        return (pltpu.make_async_copy(cbk_ref.at[layer, phys, row], kb_buf.at[:, j, :], sem.at[2]),
                pltpu.make_async_copy(cbv_ref.at[layer, phys, row], vb_buf.at[:, j, :], sem.at[3]))

    moba_keys = [(h, r, half) for h in range(h_n) for r in range(MOBA_TOPK) for half in range(ppb)]
    for key in moba_keys:
        for c in moba_copies(*key):
            c.start()

    def start_row(j, _):
        for c in dsa_copies(j):
            c.start()
        return 0

    lax.fori_loop(0, k_sel, start_row, 0)
    for key in moba_keys:
        for c in moba_copies(*key):
            c.wait()

    def wait_row(j, _):
        for c in dsa_copies(j):
            c.wait()
        return 0

    lax.fori_loop(0, k_sel, wait_row, 0)

    @pl.when(pos_ref[b, k_sel - 1] == past)
    def _():
        for c in range(kvh_n):
            kb_buf[c, k_sel - 1:k_sel, :] = kbn_ref[c:c + 1, :]
            vb_buf[c, k_sel - 1:k_sel, :] = vbn_ref[c:c + 1, :]

    qa = qa_ref[...]
    qaf = qa.astype(F32)
    kan = kan_ref[...].astype(BF16).astype(F32)
    van = van_ref[...].astype(BF16).astype(F32)
    for h in range(h_n):
        c = h // g
        s = _dot_nt(qa, ka_buf[h].astype(BF16))[h:h + 1, :] * scale
        s_new = jnp.sum(qaf[h:h + 1, :] * kan[c:c + 1, :], axis=1, keepdims=True) * scale
        m = jnp.maximum(jnp.max(s, axis=1, keepdims=True), s_new)
        p = jnp.exp(s - m)
        p_new = jnp.exp(s_new - m)
        l = jnp.sum(p, axis=1, keepdims=True) + p_new
        pv = _dot(jnp.broadcast_to(p, (8, p.shape[1])).astype(BF16), va_buf[h].astype(BF16))[0:1, :]
        oa_ref[h:h + 1, :] = (pv + p_new * van[c:c + 1, :]) / l

    qb = qb_ref[...]
    for c in range(kvh_n):
        s = _dot_nt(qb, kb_buf[c].astype(BF16)) * scale
        m = jnp.max(s, axis=1, keepdims=True)
        p = jnp.exp(s - m)
        l = jnp.sum(p, axis=1, keepdims=True)
        o = _dot(p.astype(BF16), vb_buf[c].astype(BF16)) / l
        ob_ref[c * g:(c + 1) * g, :] = o[c * g:(c + 1) * g, :]


def sample_attend(page_table, ids, pos, qa, ka_new, va_new, qb, kb_new, vb_new,
                  cache_a_k, cache_a_v, cache_b_k, cache_b_v, layer):
    bs, n_pages = page_table.shape
    ps, kvh_n = cache_a_k.shape[2], cache_a_k.shape[3]
    past = n_pages * ps
    k_sel = pos.shape[1]
    h_n = qa.shape[1]
    per_seq = lambda shape: pl.BlockSpec((None,) + shape, lambda b, *_: (b,) + (0,) * len(shape))
    hbm = pl.BlockSpec(memory_space=pl.ANY)
    grid_spec = pltpu.PrefetchScalarGridSpec(
        num_scalar_prefetch=3,
        grid=(bs,),
        in_specs=[per_seq((h_n, HEAD_DIM)), per_seq((kvh_n, HEAD_DIM)), per_seq((kvh_n, HEAD_DIM)),
                  per_seq((h_n, HEAD_DIM)), per_seq((kvh_n, HEAD_DIM)), per_seq((kvh_n, HEAD_DIM)),
                  hbm, hbm, hbm, hbm],
        out_specs=[per_seq((h_n, HEAD_DIM)), per_seq((h_n, HEAD_DIM))],
        scratch_shapes=[pltpu.VMEM((h_n, MOBA_TOPK * MOBA_BLOCK, HEAD_DIM), F32),
                        pltpu.VMEM((h_n, MOBA_TOPK * MOBA_BLOCK, HEAD_DIM), F32),
                        pltpu.VMEM((kvh_n, k_sel, HEAD_DIM), F32),
                        pltpu.VMEM((kvh_n, k_sel, HEAD_DIM), F32),
                        pltpu.SemaphoreType.DMA((4,))],
    )
    return pl.pallas_call(
        functools.partial(_sample_attend_kernel, layer=layer, ps=ps, past=past, k_sel=k_sel),
        grid_spec=grid_spec,
        out_shape=[jax.ShapeDtypeStruct((bs, h_n, HEAD_DIM), F32)] * 2,
        compiler_params=_cparams(("arbitrary",)),
        name="sample_attend",
    )(page_table, ids, pos, qa, ka_new, va_new, qb, kb_new, vb_new, cache_a_k, cache_a_v, cache_b_k, cache_b_v)


def _sample_mixers(qa, ka, va, qb, kb, vb, qi, ki, wi, cache_a_k, cache_a_v, cache_b_k, cache_b_v,
                   cache_b_kidx, layer, page_table):
    bs = qa.shape[0]
    ps = cache_a_k.shape[2]
    n_pages = page_table.shape[1]
    past = n_pages * ps
    assert MOBA_BLOCK % ps == 0 and past % MOBA_BLOCK == 0 and n_pages % SAMPLE_PPS == 0
    assert MOBA_TOPK <= past // MOBA_BLOCK <= NBP_PAD and SAMPLE_PPS % (MOBA_BLOCK // ps) == 0
    k_sel = min(DSA_TOPK, (past + 1) // 4)
    assert k_sel == DSA_TOPK
    r3 = lambda a: a.reshape(bs, -1, HEAD_DIM)
    wi3 = (wi[:, :IDX_HEADS] * (IDX_HEADS * HEAD_DIM) ** -0.5).reshape(bs, IDX_HEADS, 1)
    iscore, inew, ids = sample_scan(page_table, cache_a_k, cache_b_kidx, layer, r3(qi), wi3,
                                    ki.reshape(bs, 1, HEAD_DIM), r3(qa))
    pos = sample_select(iscore, inew, k_sel, past).reshape(bs, k_sel)
    ids = ids[:, :, :MOBA_TOPK].reshape(bs, -1)
    o_a, o_b = sample_attend(page_table, ids, pos, r3(qa), r3(ka), r3(va), r3(qb), r3(kb), r3(vb),
                             cache_a_k, cache_a_v, cache_b_k, cache_b_v, layer)
    return o_a.reshape(bs, -1), o_b.reshape(bs, -1)


N_HEADS_A = 8
N_KV_A = 4
N_HEADS_B = 8
N_KV_B = 4
PLE_PAD_ROWS = 128


def _split_w_in(w):
    d = HEAD_DIM
    sizes = (N_HEADS_A * d, N_KV_A * d, N_KV_A * d, N_HEADS_B * d, N_KV_B * d, N_KV_B * d,
             IDX_HEADS * d, d, IDX_HEADS, w.shape[0], w.shape[0])
    offs = np.concatenate([[0], np.cumsum(sizes)])
    qa, ka, va, qb, kb, vb, qi, ki, wi, ga, gb = (w[:, offs[n]:offs[n + 1]] for n in range(len(sizes)))
    wi = jnp.pad(wi, ((0, 0), (0, LANES - IDX_HEADS)))
    c = lambda *a: jnp.concatenate(a, axis=1).astype(BF16)
    return dict(qq=c(qi, qa, qb), gates=c(ga, gb), ka=c(ka), va=c(va), kb=c(kb), vb=c(vb), ki=c(ki), wi=c(wi))


def _project_all(h, w, cos, sin, tm):
    kw = dict(tm=tm, tn=512)
    return dict(
        qq=project(h, w['qq'], cos, sin, rope=True, out_dtype=BF16, **kw),
        gates=project(h, w['gates'], cos, sin, rope=False, out_dtype=BF16, **kw),
        ka=project(h, w['ka'], cos, sin, rope=True, out_dtype=F32, **kw),
        va=project(h, w['va'], cos, sin, rope=False, out_dtype=F32, **kw),
        kb=project(h, w['kb'], cos, sin, rope=True, out_dtype=F32, **kw),
        vb=project(h, w['vb'], cos, sin, rope=False, out_dtype=F32, **kw),
        ki=project(h, w['ki'], cos, sin, rope=True, out_dtype=F32, **kw),
        wi=project(h, w['wi'], cos, sin, rope=False, out_dtype=F32, **kw),
    )


def _post_mixer(x1, p, lw, g_final, final_norm, tm_topk, tm_dense, tm_ple):
    ht, s1t, s2t, st = peer_topk(x1, lw['g_ffn'], lw['w_peer_q'], lw['keys1'], lw['keys2'], tm_topk)
    pt = peer_dense(ht, s1t, s2t, st, lw['u'], lw['vt'], tm_dense, 1024)
    return ple(x1, pt, p, lw['g_ple'], lw['w_ple_gate'], lw['w_ple_proj'], g_final, tm_ple, final_norm)


def kernel(x_prompt, x_sample, cache_a_k, cache_a_v, cache_b_k, cache_b_v, cache_b_kidx, page_table,
           p_prompt, p_sample, g_attn, w_in, w_branch_a, w_branch_b, w_out, g_ffn, w_peer_q,
           peer_keys1, peer_keys2, peer_u, peer_v, g_ple, w_ple_gate, w_ple_proj, g_final):
    B, T, D = x_prompt.shape
    bs, ts, _ = x_sample.shape
    assert ts == 1
    depth = w_in.shape[0]
    past = page_table.shape[1] * cache_a_k.shape[2]
    mp = B * T
    ms = PLE_PAD_ROWS
    qi_w = IDX_HEADS * HEAD_DIM
    qa_w = N_HEADS_A * HEAD_DIM

    cos_p, sin_p = rope_tables(jnp.arange(T, dtype=jnp.int32))
    cos_p, sin_p = jnp.tile(cos_p, (B, 1)), jnp.tile(sin_p, (B, 1))
    cos_s, sin_s = rope_tables(jnp.full((ms,), past, jnp.int32))

    xp = x_prompt.reshape(mp, D)
    xs = jnp.pad(x_sample.reshape(bs, D), ((0, ms - bs), (0, 0)))
    rows_p = [[] for _ in range(5)]
    rows_s = [[] for _ in range(5)]
    for i in range(depth):
        w = _split_w_in(w_in[i])
        lw = dict(g_ffn=g_ffn[i], w_peer_q=w_peer_q[i].astype(BF16), keys1=peer_keys1[i].astype(BF16),
                  keys2=peer_keys2[i].astype(BF16), u=cast_bf16(peer_u[i], 1024), vt=cast_bf16(peer_v[i], 1024, transpose=True),
                  g_ple=g_ple[i], w_ple_gate=w_ple_gate[i].astype(BF16), w_ple_proj=w_ple_proj[i].astype(BF16))
        w_a, w_b, w_o = w_branch_a[i].astype(BF16), w_branch_b[i].astype(BF16), w_out[i].astype(BF16)
        last = i == depth - 1

        sr = _project_all(rmsnorm_bf16(xs, g_attn[i], ms), w, cos_s, sin_s, ms)
        f = lambda a: a[:bs]
        qq = f(sr['qq'])
        so_a, so_b = _sample_mixers(
            qq[:, qi_w:qi_w + qa_w], f(sr['ka']), f(sr['va']), qq[:, qi_w + qa_w:], f(sr['kb']), f(sr['vb']),
            qq[:, :qi_w], f(sr['ki']), f(sr['wi']),
            cache_a_k, cache_a_v, cache_b_k, cache_b_v, cache_b_kidx, i, page_table)
        padr = lambda a: jnp.pad(a, ((0, ms - bs), (0, 0))).astype(BF16)
        x1s = merge(padr(so_a), padr(so_b), sr['gates'], xs, w_a, w_b, w_o, ms)
        ps = jnp.pad(p_sample[i].reshape(bs, -1), ((0, ms - bs), (0, 0)))
        xs = _post_mixer(x1s, ps, lw, g_final, last, ms, ms, ms)
        for lst, name in zip(rows_s, ('ka', 'va', 'kb', 'vb', 'ki')):
            r = sr[name][:bs]
            lst.append(r.reshape((bs, 1, -1, HEAD_DIM) if name != 'ki' else (bs, 1, HEAD_DIM)))

        pr = _project_all(rmsnorm_bf16(xp, g_attn[i], 512), w, cos_p, sin_p, 1024)
        o_a = moba_prompt(pr['qq'], pr['ka'], pr['va'], B, T, N_HEADS_A, q_col0=qi_w)
        o_b = dsa_prompt(pr['qq'], pr['wi'], pr['qq'], pr['ki'], pr['kb'], pr['vb'], B, T, N_HEADS_B,
                         qi_col0=0, q_col0=qi_w + qa_w)
        x1 = merge(o_a, o_b, pr['gates'], xp, w_a, w_b, w_o, 256)
        xp = _post_mixer(x1, p_prompt[i].reshape(mp, -1), lw, g_final, last, 256, 512, 256)
        for lst, name in zip(rows_p, ('ka', 'va', 'kb', 'vb', 'ki')):
            lst.append(pr[name].reshape((B, T, -1, HEAD_DIM) if name != 'ki' else (B, T, HEAD_DIM)))

    y_prompt = xp.reshape(B, T, D)
    y_sample = xs[:bs].reshape(bs, 1, D)
    return (y_prompt, y_sample) + tuple(jnp.stack(r) for r in rows_p) + tuple(jnp.stack(r) for r in rows_s)
```

```python
import functools
import math

import jax
import jax.numpy as jnp
import numpy as np
from jax import lax
from jax.experimental import pallas as pl
from jax.experimental.pallas import tpu as pltpu

F32 = jnp.float32
BF16 = jnp.bfloat16
NEG_INF = float("-inf")

LANES = 128
HEAD_DIM = 128
MOBA_BLOCK = 256
MOBA_TOPK = 3
DSA_TOPK = 256
IDX_HEADS = 16
PEER_HEADS = 8
PEER_N_KEYS = 128
PEER_TOPK = 16
ROPE_THETA = 10000.0
RMS_EPS = 1e-6
VMEM_LIMIT = 56 * 1024 * 1024


def _cparams(sem):
    return pltpu.CompilerParams(dimension_semantics=sem, vmem_limit_bytes=VMEM_LIMIT)


def _dot_nt(a, b, precision=None):
    return lax.dot_general(a, b, (((1,), (1,)), ((), ())),
                           preferred_element_type=F32, precision=precision)


def _dot(a, b):
    return jnp.dot(a, b, preferred_element_type=F32)


def _rms(x, g):
    return x * lax.rsqrt(jnp.mean(x * x, axis=-1, keepdims=True) + RMS_EPS) * g


def _rmsnorm_kernel(x_ref, g_ref, o_ref):
    o_ref[...] = _rms(x_ref[...], g_ref[...]).astype(o_ref.dtype)


def rmsnorm_bf16(x, g, tm):
    M, D = x.shape
    return pl.pallas_call(
        _rmsnorm_kernel,
        grid=(M // tm,),
        in_specs=[pl.BlockSpec((tm, D), lambda i: (i, 0)),
                  pl.BlockSpec((1, D), lambda i: (0, 0))],
        out_specs=pl.BlockSpec((tm, D), lambda i: (i, 0)),
        out_shape=jax.ShapeDtypeStruct((M, D), BF16),
        compiler_params=_cparams(("parallel",)),
        name="rmsnorm",
    )(x, g.reshape(1, D))


def _cast_kernel(x_ref, o_ref, *, transpose):
    x = x_ref[...]
    o_ref[...] = (x.T if transpose else x).astype(o_ref.dtype)


def cast_bf16(x, tr, transpose=False):
    R, C = x.shape
    return pl.pallas_call(
        functools.partial(_cast_kernel, transpose=transpose),
        grid=(R // tr,),
        in_specs=[pl.BlockSpec((tr, C), lambda i: (i, 0))],
        out_specs=pl.BlockSpec((C, tr), lambda i: (0, i)) if transpose else pl.BlockSpec((tr, C), lambda i: (i, 0)),
        out_shape=jax.ShapeDtypeStruct((C, R) if transpose else (R, C), BF16),
        compiler_params=_cparams(("parallel",)),
        name="cast_t" if transpose else "cast",
    )(x)


def _proj_kernel(h_ref, w_ref, cos_ref, sin_ref, o_ref, *, rope, scale):
    acc = _dot(h_ref[...], w_ref[...])
    if scale != 1.0:
        acc = acc * scale
    if rope:
        cos = cos_ref[...]
        sin = sin_ref[...]
        for c in range(acc.shape[1] // HEAD_DIM):
            a = acc[:, c * HEAD_DIM:(c + 1) * HEAD_DIM]
            r = a * cos + pltpu.roll(a, HEAD_DIM // 2, 1) * sin
            o_ref[:, c * HEAD_DIM:(c + 1) * HEAD_DIM] = r.astype(o_ref.dtype)
    else:
        o_ref[...] = acc.astype(o_ref.dtype)


def project(h, w, cos, sin, *, rope, out_dtype, tm, tn, scale=1.0):
    M, K = h.shape
    N = w.shape[1]
    tn = min(tn, N)
    return pl.pallas_call(
        functools.partial(_proj_kernel, rope=rope, scale=scale),
        grid=(M // tm, N // tn),
        in_specs=[pl.BlockSpec((tm, K), lambda i, j: (i, 0)),
                  pl.BlockSpec((K, tn), lambda i, j: (0, j)),
                  pl.BlockSpec((tm, HEAD_DIM), lambda i, j: (i, 0)),
                  pl.BlockSpec((tm, HEAD_DIM), lambda i, j: (i, 0))],
        out_specs=pl.BlockSpec((tm, tn), lambda i, j: (i, j)),
        out_shape=jax.ShapeDtypeStruct((M, N), out_dtype),
        compiler_params=_cparams(("parallel", "arbitrary")),
        name="project_rope" if rope else "project",
    )(h, w, cos, sin)


def rope_tables(pos):
    half = HEAD_DIM // 2
    inv = ROPE_THETA ** (-jnp.arange(half, dtype=F32) / half)
    ang = pos.astype(F32)[:, None] * inv[None, :]
    cos = jnp.cos(ang)
    sin = jnp.sin(ang)
    return jnp.concatenate([cos, cos], axis=-1), jnp.concatenate([-sin, sin], axis=-1)


def _moba_prompt_kernel(q_ref, k_ref, v_ref, o_ref, kb_scr, vt_scr, mean_scr, sel_scr, *, nb):
    i = pl.program_id(2)
    blk = MOBA_BLOCK
    g = q_ref.shape[1] // HEAD_DIM
    nq = g * blk

    @pl.when(i == 0)
    def _():
        for j in range(nb):
            kj = k_ref[j * blk:(j + 1) * blk, :]
            mean_scr[j:j + 1, :] = jnp.sum(kj, axis=0, keepdims=True) / blk
            kb_scr[j] = kj.astype(BF16)
            vt_scr[j] = v_ref[j * blk:(j + 1) * blk, :].T.astype(BF16)

    q = q_ref[...]
    qs = jnp.concatenate([q[:, h * HEAD_DIM:(h + 1) * HEAD_DIM] for h in range(g)], axis=0)

    gate = _dot_nt(mean_scr[...], qs.astype(F32), precision=lax.Precision.HIGHEST)
    row = lax.broadcasted_iota(jnp.int32, (nb, nq), 0)
    valid = row < i
    gate = jnp.where(valid, gate, NEG_INF)
    rank = jnp.zeros((nb, nq), jnp.int32)
    for jp in range(nb):
        gj = gate[jp:jp + 1, :]
        beats = (gj > gate) | ((gj == gate) & (jp < row))
        rank = rank + beats.astype(jnp.int32)
    sel_scr[...] = jnp.where(valid & (rank < MOBA_TOPK), 1.0, 0.0).astype(F32)

    scale = HEAD_DIM ** -0.5
    s = _dot_nt(kb_scr[i], qs) * scale
    kpos = lax.broadcasted_iota(jnp.int32, (blk, nq), 0)
    qpos = lax.broadcasted_iota(jnp.int32, (blk, nq), 1) % blk
    s = jnp.where(kpos <= qpos, s, NEG_INF)
    m0 = jnp.max(s, axis=0, keepdims=True)
    p = jnp.exp(s - m0)
    l0 = jnp.sum(p, axis=0, keepdims=True)
    acc0 = _dot(vt_scr[i], p.astype(BF16))

    def body(j, carry):
        m, l, acc = carry
        sj = _dot_nt(kb_scr[j], qs) * scale
        selj = sel_scr[pl.ds(j, 1), :] > 0.5
        sj = jnp.where(selj, sj, NEG_INF)
        m_new = jnp.maximum(m, jnp.max(sj, axis=0, keepdims=True))
        pj = jnp.exp(sj - m_new)
        alpha = jnp.exp(m - m_new)
        l = alpha * l + jnp.sum(pj, axis=0, keepdims=True)
        acc = alpha * acc + _dot(vt_scr[j], pj.astype(BF16))
        return m_new, l, acc

    _, l, acc = lax.fori_loop(0, i, body, (m0, l0, acc0))
    out = acc / l
    for h in range(g):
        o_ref[:, h * HEAD_DIM:(h + 1) * HEAD_DIM] = out[:, h * blk:(h + 1) * blk].T.astype(o_ref.dtype)


def moba_prompt(q, k, v, B, T, H, q_col0=0):
    KVH = k.shape[1] // HEAD_DIM
    g = H // KVH
    nb = T // MOBA_BLOCK
    qb0 = q_col0 // (g * HEAD_DIM)
    return pl.pallas_call(
        functools.partial(_moba_prompt_kernel, nb=nb),
        grid=(B, KVH, nb),
        in_specs=[pl.BlockSpec((MOBA_BLOCK, g * HEAD_DIM), lambda b, k_, i: (b * nb + i, qb0 + k_)),
                  pl.BlockSpec((T, HEAD_DIM), lambda b, k_, i: (b, k_)),
                  pl.BlockSpec((T, HEAD_DIM), lambda b, k_, i: (b, k_))],
        out_specs=pl.BlockSpec((MOBA_BLOCK, g * HEAD_DIM), lambda b, k_, i: (b * nb + i, k_)),
        out_shape=jax.ShapeDtypeStruct((B * T, H * HEAD_DIM), BF16),
        scratch_shapes=[pltpu.VMEM((nb, MOBA_BLOCK, HEAD_DIM), BF16),
                        pltpu.VMEM((nb, HEAD_DIM, MOBA_BLOCK), BF16),
                        pltpu.VMEM((nb, HEAD_DIM), F32),
                        pltpu.VMEM((nb, g * MOBA_BLOCK), F32)],
        compiler_params=_cparams(("parallel", "parallel", "arbitrary")),
        name="moba_prompt",
    )(q, k, v)


def _ordered_key(x):
    u = pltpu.bitcast(x, jnp.int32)
    return u ^ ((u >> 31) & jnp.int32(0x7FFFFFFF))


def _dsa_prompt_kernel(qi_ref, wi_ref, q_ref, ki_ref, k_ref, v_ref, o_ref,
                       kib_scr, kb_scr, vt_scr, key_scr, sel_scr, *, nb, k_sel):
    i = pl.program_id(1)
    tq = qi_ref.shape[0]
    kvh_n = k_ref.shape[1] // HEAD_DIM
    h_n = q_ref.shape[1] // HEAD_DIM
    g = h_n // kvh_n
    hi_n = qi_ref.shape[1] // HEAD_DIM
    int_min = jnp.int32(-2 ** 31)

    @pl.when(i == 0)
    def _():
        for j in range(nb):
            rows = slice(j * tq, (j + 1) * tq)
            kib_scr[j] = ki_ref[rows, :].astype(BF16)
            for c in range(kvh_n):
                cols = slice(c * HEAD_DIM, (c + 1) * HEAD_DIM)
                kb_scr[c * nb + j] = k_ref[rows, cols].astype(BF16)
                vt_scr[c * nb + j] = v_ref[rows, cols].T.astype(BF16)

    wit = wi_ref[...].T * (hi_n * HEAD_DIM) ** -0.5
    qi = qi_ref[...]

    kpos = lax.broadcasted_iota(jnp.int32, (tq, tq), 0)
    qpos = lax.broadcasted_iota(jnp.int32, (tq, tq), 1)
    causal = kpos <= qpos

    def score_blk(j, _):
        kib = kib_scr[j]
        acc = jnp.zeros((tq, tq), F32)
        for h in range(hi_n):
            r = _dot_nt(kib, qi[:, h * HEAD_DIM:(h + 1) * HEAD_DIM])
            acc = acc + jnp.maximum(r, 0.0) * wit[h:h + 1, :]
        key = _ordered_key(acc)
        key = jnp.where((j < i) | causal, key, int_min)
        key_scr[j] = key
        return 0

    lax.fori_loop(0, i + 1, score_blk, 0)

    def count(pred_fn):
        def blk(j, c):
            m = pred_fn(key_scr[j], j)
            return c + jnp.sum(m.astype(jnp.int32).reshape(tq // 8, 8, tq), axis=0)
        c8 = lax.fori_loop(0, i + 1, blk, jnp.zeros((8, tq), jnp.int32))
        return jnp.sum(c8, axis=0, keepdims=True)

    def bit_step(it, t_u):
        bit = lax.shift_left(jnp.int32(1), 31 - it)
        cand_u = t_u | bit
        cand = cand_u ^ int_min
        c = count(lambda kk, j: kk >= cand)
        return jnp.where(c >= k_sel, cand_u, t_u)

    t_u = lax.fori_loop(0, 32, bit_step, jnp.zeros((1, tq), jnp.int32))
    thr = t_u ^ int_min
    need = k_sel - count(lambda kk, j: kk > thr)

    def idx_of(j):
        return lax.broadcasted_iota(jnp.int32, (tq, tq), 0) + j * tq

    n_bits = max(1, int(math.ceil(math.log2(nb * tq))))

    def tie_step(it, y):
        cand = y | lax.shift_left(jnp.int32(1), n_bits - 1 - it)
        c = count(lambda kk, j: (kk == thr) & (idx_of(j) < cand))
        return jnp.where(c < need, cand, y)

    surplus = jnp.max(count(lambda kk, j: kk == thr) - need)
    y = lax.cond(surplus > 0,
                 lambda: lax.fori_loop(0, n_bits, tie_step, jnp.zeros((1, tq), jnp.int32)),
                 lambda: jnp.full((1, tq), 2 ** n_bits - 1, jnp.int32))

    def sel_blk(j, _):
        kk = key_scr[j]
        s = (kk > thr) | ((kk == thr) & (idx_of(j) <= y))
        s = s & (kk > int_min)
        sel_scr[j] = jnp.where(s, 1.0, 0.0).astype(F32)
        return 0

    lax.fori_loop(0, i + 1, sel_blk, 0)

    scale = HEAD_DIM ** -0.5
    q = q_ref[...]
    for c in range(kvh_n):
        qs = jnp.concatenate([q[:, h * HEAD_DIM:(h + 1) * HEAD_DIM] for h in range(c * g, (c + 1) * g)], axis=0)

        def body(j, carry, c=c, qs=qs):
            m, l, acc = carry
            s = _dot_nt(kb_scr[c * nb + j], qs) * scale
            sel = jnp.concatenate([sel_scr[j]] * g, axis=1) > 0.5
            m_new = jnp.maximum(m, jnp.max(jnp.where(sel, s, -1e30), axis=0, keepdims=True))
            p = jnp.where(sel, jnp.exp(s - m_new), 0.0)
            alpha = jnp.exp(m - m_new)
            l = alpha * l + jnp.sum(p, axis=0, keepdims=True)
            acc = alpha * acc + _dot(vt_scr[c * nb + j], p.astype(BF16))
            return m_new, l, acc

        init = (jnp.full((1, g * tq), -1e30, F32), jnp.zeros((1, g * tq), F32),
                jnp.zeros((HEAD_DIM, g * tq), F32))
        _, l, acc = lax.fori_loop(0, i + 1, body, init)
        out = acc / l
        for hh in range(g):
            h = c * g + hh
            o_ref[:, h * HEAD_DIM:(h + 1) * HEAD_DIM] = out[:, hh * tq:(hh + 1) * tq].T.astype(o_ref.dtype)


def dsa_prompt(qi, wi, q, ki, k, v, B, T, H, qi_col0=0, q_col0=0):
    tq = 256
    nb = T // tq
    k_sel = min(DSA_TOPK, T // 4)
    kvh_n = k.shape[1] // HEAD_DIM
    wqi = IDX_HEADS * HEAD_DIM
    wq = H * HEAD_DIM
    qib, qb = qi_col0 // wqi, q_col0 // wq
    return pl.pallas_call(
        functools.partial(_dsa_prompt_kernel, nb=nb, k_sel=k_sel),
        grid=(B, nb),
        in_specs=[pl.BlockSpec((tq, wqi), lambda b, i: (b * nb + i, qib)),
                  pl.BlockSpec((tq, LANES), lambda b, i: (b * nb + i, 0)),
                  pl.BlockSpec((tq, wq), lambda b, i: (b * nb + i, qb)),
                  pl.BlockSpec((T, HEAD_DIM), lambda b, i: (b, 0)),
                  pl.BlockSpec((T, k.shape[1]), lambda b, i: (b, 0)),
                  pl.BlockSpec((T, v.shape[1]), lambda b, i: (b, 0))],
        out_specs=pl.BlockSpec((tq, wq), lambda b, i: (b * nb + i, 0)),
        out_shape=jax.ShapeDtypeStruct((B * T, wq), BF16),
        scratch_shapes=[pltpu.VMEM((nb, tq, HEAD_DIM), BF16),
                        pltpu.VMEM((kvh_n * nb, tq, HEAD_DIM), BF16),
                        pltpu.VMEM((kvh_n * nb, HEAD_DIM, tq), BF16),
                        pltpu.VMEM((nb, tq, tq), jnp.int32),
                        pltpu.VMEM((nb, tq, tq), F32)],
        compiler_params=_cparams(("parallel", "arbitrary")),
        name="dsa_prompt",
    )(qi, wi, q, ki, k, v)


def _resident(shape):
    nd = len(shape)
    return pl.BlockSpec(shape, lambda *_: (0,) * nd, pipeline_mode=pl.Buffered(1))


def _merge_kernel(oa_ref, ob_ref, ga_ref, gb_ref, x_ref, wa_ref, wb_ref, wo_ref, o_ref):
    ya = _dot(oa_ref[...], wa_ref[...])
    yb = _dot(ob_ref[...], wb_ref[...])
    z = jax.nn.sigmoid(ga_ref[...].astype(F32)) * ya + jax.nn.sigmoid(gb_ref[...].astype(F32)) * yb
    o_ref[...] = x_ref[...] + _dot(z.astype(BF16), wo_ref[...])


def merge(o_a, o_b, gates, x, w_a, w_b, w_o, tm):
    M, D = x.shape
    return pl.pallas_call(
        _merge_kernel,
        grid=(M // tm,),
        in_specs=[pl.BlockSpec((tm, o_a.shape[1]), lambda i: (i, 0)),
                  pl.BlockSpec((tm, o_b.shape[1]), lambda i: (i, 0)),
                  pl.BlockSpec((tm, D), lambda i: (i, 0)),
                  pl.BlockSpec((tm, D), lambda i: (i, 1)),
                  pl.BlockSpec((tm, D), lambda i: (i, 0)),
                  _resident(w_a.shape), _resident(w_b.shape), _resident(w_o.shape)],
        out_specs=pl.BlockSpec((tm, D), lambda i: (i, 0)),
        out_shape=jax.ShapeDtypeStruct((M, D), F32),
        compiler_params=_cparams(("parallel",)),
        name="merge",
    )(o_a, o_b, gates, gates, x, w_a, w_b, w_o)


def _top_values(x, n, order=None, exact=True):
    vals = []
    if exact:
        if order is None:
            order = lax.broadcasted_iota(jnp.int32, x.shape, 0)
        big = jnp.int32(2 ** 30)
        for _ in range(n):
            m = jnp.max(x, axis=0, keepdims=True)
            vals.append(m)
            first = jnp.min(jnp.where(x == m, order, big), axis=0, keepdims=True)
            x = jnp.where(order == first, NEG_INF, x)
        return jnp.concatenate(vals, axis=0), jnp.zeros((1, x.shape[1]), jnp.int32)
    y = x
    for _ in range(n):
        m = jnp.max(y, axis=0, keepdims=True)
        vals.append(m)
        y = jnp.where(y == m, NEG_INF, y)
    removed = (jnp.sum(jnp.where(y == NEG_INF, 1, 0), axis=0, keepdims=True)
               - jnp.sum(jnp.where(x == NEG_INF, 1, 0), axis=0, keepdims=True))
    return jnp.concatenate(vals, axis=0), jnp.abs(removed - n)


def _pair_candidates(v1, v2):
    k = PEER_TOPK
    tm = v1.shape[1]
    sub = lax.broadcasted_iota(jnp.int32, (8, tm), 0)
    vals = [v1[0:1, :] + v2]
    order = [lax.broadcasted_iota(jnp.int32, (k, tm), 0)]
    for a in range(1, 8):
        vals.append(jnp.where(sub < k // (a + 1), v1[a:a + 1, :] + v2[0:8, :], NEG_INF))
        order.append(sub + a * k)
    vals.append(v1[8:16, :] + v2[0:1, :])
    order.append((sub + 8) * k)
    return jnp.concatenate(vals, axis=0), jnp.concatenate(order, axis=0)


def _peer_topk_kernel(x_ref, g_ref, wq_ref, k1_ref, k2_ref, ht_ref, s1_ref, s2_ref, st_ref, q_scr):
    tm = x_ref.shape[0]
    hn = x_ref[...]
    hn = _rms(hn, g_ref[...])
    ht_ref[...] = hn.T.astype(BF16)
    q = _dot(hn.astype(BF16), wq_ref[...]).astype(BF16)
    kd = 2 * PEER_N_KEYS
    for hd in range(PEER_HEADS):
        q_scr[hd] = q[:, hd * kd:(hd + 1) * kd]

    def head(hd, _):
        qh = q_scr[hd]
        s1 = _dot_nt(k1_ref[hd], qh[:, :PEER_N_KEYS])
        s2 = _dot_nt(k2_ref[hd], qh[:, PEER_N_KEYS:])
        s1_ref[hd] = s1
        s2_ref[hd] = s2

        def stats(exact):
            out, bad = [], jnp.zeros((1, LANES), jnp.int32)
            for tc in range(tm // LANES):
                cols = slice(tc * LANES, (tc + 1) * LANES)
                v1, b1 = _top_values(s1[:, cols], PEER_TOPK, exact=exact)
                v2, b2 = _top_values(s2[:, cols], PEER_TOPK, exact=exact)
                cand, order = _pair_candidates(v1, v2)
                best, b3 = _top_values(cand, PEER_TOPK, order, exact=exact)
                z = jnp.sum(jnp.exp(best - best[0:1, :]), axis=0, keepdims=True)
                out.append(jnp.concatenate([best[PEER_TOPK - 1:PEER_TOPK, :], v1[0:1, :], v2[0:1, :], 1.0 / z,
                                            jnp.zeros((4, LANES), F32)], axis=0))
                bad = bad + b1 + b2 + b3
            return jnp.concatenate(out, axis=1), bad

        st, bad = stats(False)
        st_ref[hd] = lax.cond(jnp.max(bad) > 0, lambda: stats(True)[0], lambda: st)
        return 0

    lax.fori_loop(0, PEER_HEADS, head, 0)


def peer_topk(x, g, w_q, keys1, keys2, tm):
    M, D = x.shape
    hp = PEER_HEADS
    return pl.pallas_call(
        _peer_topk_kernel,
        grid=(M // tm,),
        in_specs=[pl.BlockSpec((tm, D), lambda i: (i, 0)),
                  pl.BlockSpec((1, D), lambda i: (0, 0)),
                  _resident(w_q.shape), _resident(keys1.shape), _resident(keys2.shape)],
        out_specs=[pl.BlockSpec((D, tm), lambda i: (0, i)),
                   pl.BlockSpec((hp, PEER_N_KEYS, tm), lambda i: (0, 0, i)),
                   pl.BlockSpec((hp, PEER_N_KEYS, tm), lambda i: (0, 0, i)),
                   pl.BlockSpec((hp, 8, tm), lambda i: (0, 0, i))],
        out_shape=[jax.ShapeDtypeStruct((D, M), BF16),
                   jax.ShapeDtypeStruct((hp, PEER_N_KEYS, M), F32),
                   jax.ShapeDtypeStruct((hp, PEER_N_KEYS, M), F32),
                   jax.ShapeDtypeStruct((hp, 8, M), F32)],
        scratch_shapes=[pltpu.VMEM((hp, tm, 2 * PEER_N_KEYS), BF16)],
        compiler_params=_cparams(("parallel",)),
        name="peer_topk",
    )(x, g.reshape(1, D), w_q, keys1, keys2)


def _peer_dense_kernel(ht_ref, s1_ref, s2_ref, st_ref, u_ref, vt_ref, o_ref, e1_scr, e2_scr):
    e = pl.program_id(1)
    n_i1 = u_ref.shape[0] // PEER_N_KEYS

    @pl.when(e == 0)
    def _():
        o_ref[...] = jnp.zeros_like(o_ref)
        for hd in range(PEER_HEADS):
            st = st_ref[hd]
            e1_scr[hd] = jnp.exp(s1_ref[hd] - st[1:2, :]) * st[3:4, :]
            e2_scr[hd] = jnp.exp(s2_ref[hd] - st[2:3, :])

    a = _dot(u_ref[...], ht_ref[...])
    act = 0.5 * a * (1.0 + lax.erf(a * (2.0 ** -0.5)))
    zs = []
    for r in range(n_i1):
        i1 = e * n_i1 + r
        w = jnp.zeros((PEER_N_KEYS, a.shape[1]), F32)
        for hd in range(PEER_HEADS):
            t = s1_ref[hd, pl.ds(i1, 1), :] + s2_ref[hd]
            w = w + jnp.where(t >= st_ref[hd, 0:1, :], e1_scr[hd, pl.ds(i1, 1), :] * e2_scr[hd], 0.0)
        zs.append((act[r * PEER_N_KEYS:(r + 1) * PEER_N_KEYS, :] * w).astype(BF16))
    z = jnp.concatenate(zs, axis=0)
    o_ref[...] += _dot(vt_ref[...], z)


def peer_dense(ht, s1t, s2t, stats, u, vt, tm, e_blk):
    D, M = ht.shape
    E = u.shape[0]
    hp = PEER_HEADS
    return pl.pallas_call(
        _peer_dense_kernel,
        grid=(M // tm, E // e_blk),
        in_specs=[pl.BlockSpec((D, tm), lambda i, e: (0, i)),
                  pl.BlockSpec((hp, PEER_N_KEYS, tm), lambda i, e: (0, 0, i)),
                  pl.BlockSpec((hp, PEER_N_KEYS, tm), lambda i, e: (0, 0, i)),
                  pl.BlockSpec((hp, 8, tm), lambda i, e: (0, 0, i)),
                  pl.BlockSpec((e_blk, D), lambda i, e: (e, 0)),
                  pl.BlockSpec((D, e_blk), lambda i, e: (0, e))],
        out_specs=pl.BlockSpec((D, tm), lambda i, e: (0, i)),
        out_shape=jax.ShapeDtypeStruct((D, M), F32),
        scratch_shapes=[pltpu.VMEM((hp, PEER_N_KEYS, tm), F32),
                        pltpu.VMEM((hp, PEER_N_KEYS, tm), F32)],
        compiler_params=_cparams(("parallel", "arbitrary")),
        name="peer_dense",
    )(ht, s1t, s2t, stats, u, vt)


def _ple_kernel(x_ref, pt_ref, p_ref, g_ref, wg_ref, wp_ref, gf_ref, o_ref, *, final_norm):
    x2 = x_ref[...] + pt_ref[...].T
    h = _rms(x2, g_ref[...]).astype(BF16)
    gate = jax.nn.sigmoid(_dot(h, wg_ref[...]))
    x3 = x2 + gate * _dot(p_ref[...].astype(BF16), wp_ref[...])
    if final_norm:
        x3 = _rms(x3, gf_ref[...])
    o_ref[...] = x3


def ple(x, peer_t, p, g, w_gate, w_proj, g_final, tm, final_norm):
    M, D = x.shape
    return pl.pallas_call(
        functools.partial(_ple_kernel, final_norm=final_norm),
        grid=(M // tm,),
        in_specs=[pl.BlockSpec((tm, D), lambda i: (i, 0)),
                  pl.BlockSpec((D, tm), lambda i: (0, i)),
                  pl.BlockSpec((tm, p.shape[1]), lambda i: (i, 0)),
                  pl.BlockSpec((1, D), lambda i: (0, 0)),
                  _resident(w_gate.shape), _resident(w_proj.shape),
                  pl.BlockSpec((1, D), lambda i: (0, 0))],
        out_specs=pl.BlockSpec((tm, D), lambda i: (i, 0)),
        out_shape=jax.ShapeDtypeStruct((M, D), F32),
        compiler_params=_cparams(("parallel",)),
        name="ple",
    )(x, peer_t, p, g.reshape(1, D), w_gate, w_proj, g_final.reshape(1, D))


SAMPLE_PPS = 8
NBP_PAD = 128


def _sample_scan_kernel(pt_ref, qi_ref, wi_ref, kin_ref, qa_ref, cak_ref, cki_ref, isc_ref, inew_ref, ids_ref,
                        ak_buf, ki_buf, sums_scr, sem, *, layer, pps, ps, nbp, n_chunks):
    b = pl.program_id(0)
    n_seq = pl.num_programs(0)
    kvh_n = sums_scr.shape[0]
    ppb = MOBA_BLOCK // ps
    qi = qi_ref[...]
    w = wi_ref[...]

    def copies(seq, ch, slot):
        out = []
        for r in range(pps):
            phys = pt_ref[seq, ch * pps + r]
            out.append(pltpu.make_async_copy(cak_ref.at[layer, phys], ak_buf.at[slot, r], sem.at[0, slot]))
            out.append(pltpu.make_async_copy(cki_ref.at[layer, phys], ki_buf.at[slot, r], sem.at[1, slot]))
        return out

    @pl.when(b == 0)
    def _():
        for c in copies(0, 0, 0):
            c.start()

    sums_scr[...] = jnp.zeros_like(sums_scr)
    kin = kin_ref[...].astype(BF16).astype(F32)
    r_new = jnp.sum(qi.astype(F32) * kin, axis=1, keepdims=True)
    inew_ref[...] = jnp.broadcast_to(jnp.sum(jnp.maximum(r_new, 0.0) * w, axis=0, keepdims=True), inew_ref.shape)

    def chunk(ch, _):
        slot = lax.rem(ch, 2)

        @pl.when(ch + 1 < n_chunks)
        def _():
            for c in copies(b, ch + 1, 1 - slot):
                c.start()

        @pl.when((ch + 1 == n_chunks) & (b + 1 < n_seq))
        def _():
            for c in copies(b + 1, 0, 0):
                c.start()

        for c in copies(b, ch, slot):
            c.wait()
        for r in range(pps):
            x = ak_buf[slot, r]
            n = ps
            while n > 1:
                n //= 2
                x = x[:n] + x[n:]
            blk = lax.div(ch * pps + r, ppb)
            for c in range(kvh_n):
                sums_scr[c, pl.ds(blk, 1), :] += x[0, c:c + 1, :]
            sc = _dot_nt(qi, ki_buf[slot, r].astype(BF16))
            isc_ref[pl.ds(ch * pps + r, 1), :] = jnp.sum(jnp.maximum(sc, 0.0) * w, axis=0, keepdims=True)
        return 0

    lax.fori_loop(0, n_chunks, chunk, 0)

    qa = qa_ref[...].astype(F32)
    h_n = qa.shape[0]
    g = h_n // kvh_n
    row = lax.broadcasted_iota(jnp.int32, (h_n, NBP_PAD), 0)
    lane = lax.broadcasted_iota(jnp.int32, (h_n, NBP_PAD), 1)
    gate = jnp.full((h_n, NBP_PAD), NEG_INF, F32)
    for c in range(kvh_n):
        gc = _dot_nt(qa, sums_scr[c] / MOBA_BLOCK, precision=lax.Precision.HIGHEST)
        gate = jnp.where((row >= c * g) & (row < (c + 1) * g), gc, gate)
    gate = jnp.where(lane < nbp, gate, NEG_INF)
    ids = jnp.zeros((h_n, NBP_PAD), jnp.int32)
    for r in range(MOBA_TOPK):
        m = jnp.max(gate, axis=1, keepdims=True)
        first = jnp.min(jnp.where(gate == m, lane, NBP_PAD), axis=1, keepdims=True)
        ids = jnp.where(lane == r, first, ids)
        gate = jnp.where(lane == first, NEG_INF, gate)
    ids_ref[...] = ids


def sample_scan(page_table, cache_a_k, cache_b_kidx, layer, qi, wi, ki_new, qa):
    bs, n_pages = page_table.shape
    ps, kvh_n = cache_a_k.shape[2], cache_a_k.shape[3]
    pps = SAMPLE_PPS
    nbp = n_pages * ps // MOBA_BLOCK
    h_n, hi_n = qa.shape[1], qi.shape[1]
    per_seq = lambda shape: pl.BlockSpec((None,) + shape, lambda b, pt: (b,) + (0,) * len(shape))
    hbm = pl.BlockSpec(memory_space=pl.ANY)
    grid_spec = pltpu.PrefetchScalarGridSpec(
        num_scalar_prefetch=1,
        grid=(bs,),
        in_specs=[per_seq((hi_n, HEAD_DIM)), per_seq((hi_n, 1)), per_seq((1, HEAD_DIM)), per_seq((h_n, HEAD_DIM)),
                  hbm, hbm],
        out_specs=[per_seq((n_pages, ps)), per_seq((1, LANES)), per_seq((h_n, NBP_PAD))],
        scratch_shapes=[pltpu.VMEM((2, pps, ps, kvh_n, HEAD_DIM), F32),
                        pltpu.VMEM((2, pps, ps, HEAD_DIM), F32),
                        pltpu.VMEM((kvh_n, NBP_PAD, HEAD_DIM), F32),
                        pltpu.SemaphoreType.DMA((2, 2))],
    )
    return pl.pallas_call(
        functools.partial(_sample_scan_kernel, layer=layer, pps=pps, ps=ps, nbp=nbp, n_chunks=n_pages // pps),
        grid_spec=grid_spec,
        out_shape=[jax.ShapeDtypeStruct((bs, n_pages, ps), F32),
                   jax.ShapeDtypeStruct((bs, 1, LANES), F32),
                   jax.ShapeDtypeStruct((bs, h_n, NBP_PAD), jnp.int32)],
        compiler_params=_cparams(("arbitrary",)),
        name="sample_scan",
    )(page_table, qi, wi, ki_new, qa, cache_a_k, cache_b_kidx)


def _sample_select_kernel(isc_ref, inew_ref, pos_ref, sel_scr, new_scr, rank_scr, *, k_sel, past):
    bs, n_pages, ps = isc_ref.shape
    int_min = jnp.int32(-2 ** 31)
    keys = _ordered_key(isc_ref[...])
    knew = _ordered_key(inew_ref[...][:, :, 0:1])
    idx = (lax.broadcasted_iota(jnp.int32, keys.shape, 1) * ps
           + lax.broadcasted_iota(jnp.int32, keys.shape, 2))

    def count(m_cache, m_new):
        c = jnp.sum(m_cache.astype(jnp.int32), axis=1, keepdims=True)
        return jnp.sum(c, axis=2, keepdims=True) + m_new.astype(jnp.int32)

    def bit_step(it, t_u):
        cand_u = t_u | lax.shift_left(jnp.int32(1), 31 - it)
        cand = cand_u ^ int_min
        return jnp.where(count(keys >= cand, knew >= cand) >= k_sel, cand_u, t_u)

    thr = lax.fori_loop(0, 32, bit_step, jnp.zeros((bs, 1, 1), jnp.int32)) ^ int_min
    need = k_sel - count(keys > thr, knew > thr)
    n_bits = int(math.ceil(math.log2(past + 1)))

    def tie_step(it, y):
        cand = y | lax.shift_left(jnp.int32(1), n_bits - 1 - it)
        c = count((keys == thr) & (idx < cand), (knew == thr) & (past < cand))
        return jnp.where(c < need, cand, y)

    y = lax.fori_loop(0, n_bits, tie_step, jnp.zeros((bs, 1, 1), jnp.int32))
    sel_scr[...] = jnp.where((keys > thr) | ((keys == thr) & (idx <= y)), 1.0, 0.0).astype(F32)
    new_sel = (knew > thr) | ((knew == thr) & (past <= y))
    new_scr[...] = jnp.broadcast_to(jnp.where(new_sel, 1.0, 0.0).astype(F32), new_scr.shape)

    upper = (lax.broadcasted_iota(jnp.int32, (ps, ps), 0) < lax.broadcasted_iota(jnp.int32, (ps, ps), 1))
    lower = (lax.broadcasted_iota(jnp.int32, (n_pages, n_pages), 1)
             < lax.broadcasted_iota(jnp.int32, (n_pages, n_pages), 0))
    upper = jnp.where(upper, 1.0, 0.0).astype(BF16)
    lower = jnp.where(lower, 1.0, 0.0).astype(BF16)
    r_iota = lax.broadcasted_iota(jnp.int32, (k_sel, ps), 0).astype(F32)
    slot = lax.broadcasted_iota(jnp.int32, (k_sel, ps), 1).astype(F32)
    lane = lax.broadcasted_iota(jnp.int32, (1, k_sel), 1)

    def per_seq(b, _):
        sb = sel_scr[b]
        prefix = _dot(sb.astype(BF16), upper)
        cnt = jnp.sum(sb, axis=1, keepdims=True)
        offs = _dot(lower, jnp.broadcast_to(cnt, sb.shape).astype(BF16))
        rank_scr[...] = offs + prefix

        def per_page(p, acc):
            hit = (r_iota == rank_scr[pl.ds(p, 1), :]) & (sel_scr[b, pl.ds(p, 1), :] > 0.5)
            return acc + jnp.where(hit, lax.convert_element_type(p * ps, F32) + slot, 0.0)

        acc = lax.fori_loop(0, n_pages, per_page, jnp.zeros((k_sel, ps), F32))
        tot = jnp.sum(acc.T, axis=0, keepdims=True)
        tot = tot + jnp.where(lane == k_sel - 1, new_scr[b][:, 0:1] * past, 0.0)
        pos_ref[b] = tot.astype(jnp.int32)
        return 0

    lax.fori_loop(0, bs, per_seq, 0)


def sample_select(iscore, inew, k_sel, past):
    bs, n_pages, ps = iscore.shape
    return pl.pallas_call(
        functools.partial(_sample_select_kernel, k_sel=k_sel, past=past),
        out_shape=jax.ShapeDtypeStruct((bs, 1, k_sel), jnp.int32),
        scratch_shapes=[pltpu.VMEM((bs, n_pages, ps), F32),
                        pltpu.VMEM((bs, 1, LANES), F32),
                        pltpu.VMEM((n_pages, ps), F32)],
        compiler_params=pltpu.CompilerParams(vmem_limit_bytes=VMEM_LIMIT),
        name="sample_select",
    )(iscore, inew)


def _sample_attend_kernel(pt_ref, ids_ref, pos_ref, qa_ref, kan_ref, van_ref, qb_ref, kbn_ref, vbn_ref,
                          cak_ref, cav_ref, cbk_ref, cbv_ref, oa_ref, ob_ref,
                          ka_buf, va_buf, kb_buf, vb_buf, sem, *, layer, ps, past, k_sel):
    b = pl.program_id(0)
    h_n = qa_ref.shape[0]
    kvh_n = kan_ref.shape[0]
    g = h_n // kvh_n
    ppb = MOBA_BLOCK // ps
    scale = HEAD_DIM ** -0.5

    def moba_copies(h, r, half):
        phys = pt_ref[b, ids_ref[b, h * MOBA_TOPK + r] * ppb + half]
        rows = pl.ds((r * ppb + half) * ps, ps)
        return (pltpu.make_async_copy(cak_ref.at[layer, phys, :, h // g, :], ka_buf.at[h, rows, :], sem.at[0]),
                pltpu.make_async_copy(cav_ref.at[layer, phys, :, h // g, :], va_buf.at[h, rows, :], sem.at[1]))

    def dsa_copies(j):
        s = jnp.minimum(pos_ref[b, j], past - 1)
        phys = pt_ref[b, lax.div(s, ps)]
        row = lax.rem(s, ps)
        return (pltpu.make_async_copy(cbk_ref.at[layer, phys, row], kb_buf.at[:, j, :], sem.at[2]),
                pltpu.make_async_copy(cbv_ref.at[layer, phys, row], vb_buf.at[:, j, :], sem.at[3]))

    moba_keys = [(h, r, half) for h in range(h_n) for r in range(MOBA_TOPK) for half in range(ppb)]
    for key in moba_keys:
        for c in moba_copies(*key):
            c.start()

    def start_row(j, _):
        for c in dsa_copies(j):
            c.start()
        return 0

    lax.fori_loop(0, k_sel, start_row, 0)
    for key in moba_keys:
        for c in moba_copies(*key):
            c.wait()

    def wait_row(j, _):
        for c in dsa_copies(j):
            c.wait()
        return 0

    lax.fori_loop(0, k_sel, wait_row, 0)

    @pl.when(pos_ref[b, k_sel - 1] == past)
    def _():
        for c in range(kvh_n):
            kb_buf[c, k_sel - 1:k_sel, :] = kbn_ref[c:c + 1, :]
            vb_buf[c, k_sel - 1:k_sel, :] = vbn_ref[c:c + 1, :]

    qa = qa_ref[...]
    qaf = qa.astype(F32)
    kan = kan_ref[...].astype(BF16).astype(F32)
    van = van_ref[...].astype(BF16).astype(F32)
    for h in range(h_n):
        c = h // g
        s = _dot_nt(qa, ka_buf[h].astype(BF16))[h:h + 1, :] * scale
        s_new = jnp.sum(qaf[h:h + 1, :] * kan[c:c + 1, :], axis=1, keepdims=True) * scale
        m = jnp.maximum(jnp.max(s, axis=1, keepdims=True), s_new)
        p = jnp.exp(s - m)
        p_new = jnp.exp(s_new - m)
        l = jnp.sum(p, axis=1, keepdims=True) + p_new
        pv = _dot(jnp.broadcast_to(p, (8, p.shape[1])).astype(BF16), va_buf[h].astype(BF16))[0:1, :]
        oa_ref[h:h + 1, :] = (pv + p_new * van[c:c + 1, :]) / l

    qb = qb_ref[...]
    for c in range(kvh_n):
        s = _dot_nt(qb, kb_buf[c].astype(BF16)) * scale
        m = jnp.max(s, axis=1, keepdims=True)
        p = jnp.exp(s - m)
        l = jnp.sum(p, axis=1, keepdims=True)
        o = _dot(p.astype(BF16), vb_buf[c].astype(BF16)) / l
        ob_ref[c * g:(c + 1) * g, :] = o[c * g:(c + 1) * g, :]


def sample_attend(page_table, ids, pos, qa, ka_new, va_new, qb, kb_new, vb_new,
                  cache_a_k, cache_a_v, cache_b_k, cache_b_v, layer):
    bs, n_pages = page_table.shape
    ps, kvh_n = cache_a_k.shape[2], cache_a_k.shape[3]
    past = n_pages * ps
    k_sel = pos.shape[1]
    h_n = qa.shape[1]
    per_seq = lambda shape: pl.BlockSpec((None,) + shape, lambda b, *_: (b,) + (0,) * len(shape))
    hbm = pl.BlockSpec(memory_space=pl.ANY)
    grid_spec = pltpu.PrefetchScalarGridSpec(
        num_scalar_prefetch=3,
        grid=(bs,),
        in_specs=[per_seq((h_n, HEAD_DIM)), per_seq((kvh_n, HEAD_DIM)), per_seq((kvh_n, HEAD_DIM)),
                  per_seq((h_n, HEAD_DIM)), per_seq((kvh_n, HEAD_DIM)), per_seq((kvh_n, HEAD_DIM)),
                  hbm, hbm, hbm, hbm],
        out_specs=[per_seq((h_n, HEAD_DIM)), per_seq((h_n, HEAD_DIM))],
        scratch_shapes=[pltpu.VMEM((h_n, MOBA_TOPK * MOBA_BLOCK, HEAD_DIM), F32),
                        pltpu.VMEM((h_n, MOBA_TOPK * MOBA_BLOCK, HEAD_DIM), F32),
                        pltpu.VMEM((kvh_n, k_sel, HEAD_DIM), F32),
                        pltpu.VMEM((kvh_n, k_sel, HEAD_DIM), F32),
                        pltpu.SemaphoreType.DMA((4,))],
    )
    return pl.pallas_call(
        functools.partial(_sample_attend_kernel, layer=layer, ps=ps, past=past, k_sel=k_sel),
        grid_spec=grid_spec,
        out_shape=[jax.ShapeDtypeStruct((bs, h_n, HEAD_DIM), F32)] * 2,
        compiler_params=_cparams(("arbitrary",)),
        name="sample_attend",
    )(page_table, ids, pos, qa, ka_new, va_new, qb, kb_new, vb_new, cache_a_k, cache_a_v, cache_b_k, cache_b_v)


def _sample_mixers(qa, ka, va, qb, kb, vb, qi, ki, wi, cache_a_k, cache_a_v, cache_b_k, cache_b_v,
                   cache_b_kidx, layer, page_table):
    bs = qa.shape[0]
    ps = cache_a_k.shape[2]
    n_pages = page_table.shape[1]
    past = n_pages * ps
    assert MOBA_BLOCK % ps == 0 and past % MOBA_BLOCK == 0 and n_pages % (2 * SAMPLE_PPS) == 0
    assert MOBA_TOPK <= past // MOBA_BLOCK <= NBP_PAD and SAMPLE_PPS % (MOBA_BLOCK // ps) == 0
    k_sel = min(DSA_TOPK, (past + 1) // 4)
    assert k_sel == DSA_TOPK
    r3 = lambda a: a.reshape(bs, -1, HEAD_DIM)
    wi3 = (wi[:, :IDX_HEADS] * (IDX_HEADS * HEAD_DIM) ** -0.5).reshape(bs, IDX_HEADS, 1)
    iscore, inew, ids = sample_scan(page_table, cache_a_k, cache_b_kidx, layer, r3(qi), wi3,
                                    ki.reshape(bs, 1, HEAD_DIM), r3(qa))
    pos = sample_select(iscore, inew, k_sel, past).reshape(bs, k_sel)
    ids = ids[:, :, :MOBA_TOPK].reshape(bs, -1)
    o_a, o_b = sample_attend(page_table, ids, pos, r3(qa), r3(ka), r3(va), r3(qb), r3(kb), r3(vb),
                             cache_a_k, cache_a_v, cache_b_k, cache_b_v, layer)
    return o_a.reshape(bs, -1), o_b.reshape(bs, -1)


N_HEADS_A = 8
N_KV_A = 4
N_HEADS_B = 8
N_KV_B = 4
PLE_PAD_ROWS = 128


def _split_w_in(w):
    d = HEAD_DIM
    sizes = (N_HEADS_A * d, N_KV_A * d, N_KV_A * d, N_HEADS_B * d, N_KV_B * d, N_KV_B * d,
             IDX_HEADS * d, d, IDX_HEADS, w.shape[0], w.shape[0])
    offs = np.concatenate([[0], np.cumsum(sizes)])
    qa, ka, va, qb, kb, vb, qi, ki, wi, ga, gb = (w[:, offs[n]:offs[n + 1]] for n in range(len(sizes)))
    wi = jnp.pad(wi, ((0, 0), (0, LANES - IDX_HEADS)))
    c = lambda *a: jnp.concatenate(a, axis=1).astype(BF16)
    return dict(qq=c(qi, qa, qb), gates=c(ga, gb), ka=c(ka), va=c(va), kb=c(kb), vb=c(vb), ki=c(ki), wi=c(wi))


def _project_all(h, w, cos, sin, tm):
    kw = dict(tm=tm, tn=512)
    return dict(
        qq=project(h, w['qq'], cos, sin, rope=True, out_dtype=BF16, **kw),
        gates=project(h, w['gates'], cos, sin, rope=False, out_dtype=BF16, **kw),
        ka=project(h, w['ka'], cos, sin, rope=True, out_dtype=F32, **kw),
        va=project(h, w['va'], cos, sin, rope=False, out_dtype=F32, **kw),
        kb=project(h, w['kb'], cos, sin, rope=True, out_dtype=F32, **kw),
        vb=project(h, w['vb'], cos, sin, rope=False, out_dtype=F32, **kw),
        ki=project(h, w['ki'], cos, sin, rope=True, out_dtype=F32, **kw),
        wi=project(h, w['wi'], cos, sin, rope=False, out_dtype=F32, **kw),
    )


def _post_mixer(x1, p, lw, g_final, final_norm, tm_topk, tm_dense, tm_ple):
    ht, s1t, s2t, st = peer_topk(x1, lw['g_ffn'], lw['w_peer_q'], lw['keys1'], lw['keys2'], tm_topk)
    pt = peer_dense(ht, s1t, s2t, st, lw['u'], lw['vt'], tm_dense, 1024)
    return ple(x1, pt, p, lw['g_ple'], lw['w_ple_gate'], lw['w_ple_proj'], g_final, tm_ple, final_norm)


def kernel(x_prompt, x_sample, cache_a_k, cache_a_v, cache_b_k, cache_b_v, cache_b_kidx, page_table,
           p_prompt, p_sample, g_attn, w_in, w_branch_a, w_branch_b, w_out, g_ffn, w_peer_q,
           peer_keys1, peer_keys2, peer_u, peer_v, g_ple, w_ple_gate, w_ple_proj, g_final):
    B, T, D = x_prompt.shape
    bs, ts, _ = x_sample.shape
    assert ts == 1
    depth = w_in.shape[0]
    past = page_table.shape[1] * cache_a_k.shape[2]
    mp = B * T
    ms = PLE_PAD_ROWS
    qi_w = IDX_HEADS * HEAD_DIM
    qa_w = N_HEADS_A * HEAD_DIM

    cos_p, sin_p = rope_tables(jnp.arange(T, dtype=jnp.int32))
    cos_p, sin_p = jnp.tile(cos_p, (B, 1)), jnp.tile(sin_p, (B, 1))
    cos_s, sin_s = rope_tables(jnp.full((ms,), past, jnp.int32))

    xp = x_prompt.reshape(mp, D)
    xs = jnp.pad(x_sample.reshape(bs, D), ((0, ms - bs), (0, 0)))
    rows_p = [[] for _ in range(5)]
    rows_s = [[] for _ in range(5)]
    for i in range(depth):
        w = _split_w_in(w_in[i])
        lw = dict(g_ffn=g_ffn[i], w_peer_q=w_peer_q[i].astype(BF16), keys1=peer_keys1[i].astype(BF16),
                  keys2=peer_keys2[i].astype(BF16), u=cast_bf16(peer_u[i], 1024), vt=cast_bf16(peer_v[i], 1024, transpose=True),
                  g_ple=g_ple[i], w_ple_gate=w_ple_gate[i].astype(BF16), w_ple_proj=w_ple_proj[i].astype(BF16))
        w_a, w_b, w_o = w_branch_a[i].astype(BF16), w_branch_b[i].astype(BF16), w_out[i].astype(BF16)
        last = i == depth - 1

        sr = _project_all(rmsnorm_bf16(xs, g_attn[i], ms), w, cos_s, sin_s, ms)
        f = lambda a: a[:bs]
        qq = f(sr['qq'])
        so_a, so_b = _sample_mixers(
            qq[:, qi_w:qi_w + qa_w], f(sr['ka']), f(sr['va']), qq[:, qi_w + qa_w:], f(sr['kb']), f(sr['vb']),
            qq[:, :qi_w], f(sr['ki']), f(sr['wi']),
            cache_a_k, cache_a_v, cache_b_k, cache_b_v, cache_b_kidx, i, page_table)
        padr = lambda a: jnp.pad(a, ((0, ms - bs), (0, 0))).astype(BF16)
        x1s = merge(padr(so_a), padr(so_b), sr['gates'], xs, w_a, w_b, w_o, ms)
        ps = jnp.pad(p_sample[i].reshape(bs, -1), ((0, ms - bs), (0, 0)))
        xs = _post_mixer(x1s, ps, lw, g_final, last, ms, ms, ms)
        for lst, name in zip(rows_s, ('ka', 'va', 'kb', 'vb', 'ki')):
            r = sr[name][:bs]
            lst.append(r.reshape((bs, 1, -1, HEAD_DIM) if name != 'ki' else (bs, 1, HEAD_DIM)))

        pr = _project_all(rmsnorm_bf16(xp, g_attn[i], 512), w, cos_p, sin_p, 1024)
        o_a = moba_prompt(pr['qq'], pr['ka'], pr['va'], B, T, N_HEADS_A, q_col0=qi_w)
        o_b = dsa_prompt(pr['qq'], pr['wi'], pr['qq'], pr['ki'], pr['kb'], pr['vb'], B, T, N_HEADS_B,
                         qi_col0=0, q_col0=qi_w + qa_w)
        x1 = merge(o_a, o_b, pr['gates'], xp, w_a, w_b, w_o, 256)
        xp = _post_mixer(x1, p_prompt[i].reshape(mp, -1), lw, g_final, last, 256, 512, 256)
        for lst, name in zip(rows_p, ('ka', 'va', 'kb', 'vb', 'ki')):
            lst.append(pr[name].reshape((B, T, -1, HEAD_DIM) if name != 'ki' else (B, T, HEAD_DIM)))

    y_prompt = xp.reshape(B, T, D)
    y_sample = xs[:bs].reshape(bs, 1, D)
    return (y_prompt, y_sample) + tuple(jnp.stack(r) for r in rows_p) + tuple(jnp.stack(r) for r in rows_s)
```

```python
import functools
import math

import jax
import jax.numpy as jnp
import numpy as np
from jax import lax
from jax.experimental import pallas as pl
from jax.experimental.pallas import tpu as pltpu

F32 = jnp.float32
BF16 = jnp.bfloat16
NEG_INF = float("-inf")

LANES = 128
HEAD_DIM = 128
MOBA_BLOCK = 256
MOBA_TOPK = 3
DSA_TOPK = 256
IDX_HEADS = 16
PEER_HEADS = 8
PEER_N_KEYS = 128
PEER_TOPK = 16
ROPE_THETA = 10000.0
RMS_EPS = 1e-6
VMEM_LIMIT = 56 * 1024 * 1024


def _cparams(sem):
    return pltpu.CompilerParams(dimension_semantics=sem, vmem_limit_bytes=VMEM_LIMIT)


def _dot_nt(a, b, precision=None):
    return lax.dot_general(a, b, (((1,), (1,)), ((), ())),
                           preferred_element_type=F32, precision=precision)


def _dot(a, b):
    return jnp.dot(a, b, preferred_element_type=F32)


def _rms(x, g):
    return x * lax.rsqrt(jnp.mean(x * x, axis=-1, keepdims=True) + RMS_EPS) * g


def _rmsnorm_kernel(x_ref, g_ref, o_ref):
    o_ref[...] = _rms(x_ref[...], g_ref[...]).astype(o_ref.dtype)


def rmsnorm_bf16(x, g, tm):
    M, D = x.shape
    return pl.pallas_call(
        _rmsnorm_kernel,
        grid=(M // tm,),
        in_specs=[pl.BlockSpec((tm, D), lambda i: (i, 0)),
                  pl.BlockSpec((1, D), lambda i: (0, 0))],
        out_specs=pl.BlockSpec((tm, D), lambda i: (i, 0)),
        out_shape=jax.ShapeDtypeStruct((M, D), BF16),
        compiler_params=_cparams(("parallel",)),
        name="rmsnorm",
    )(x, g.reshape(1, D))


def _cast_kernel(x_ref, o_ref, *, transpose):
    x = x_ref[...]
    o_ref[...] = (x.T if transpose else x).astype(o_ref.dtype)


def cast_bf16(x, tr, transpose=False):
    R, C = x.shape
    return pl.pallas_call(
        functools.partial(_cast_kernel, transpose=transpose),
        grid=(R // tr,),
        in_specs=[pl.BlockSpec((tr, C), lambda i: (i, 0))],
        out_specs=pl.BlockSpec((C, tr), lambda i: (0, i)) if transpose else pl.BlockSpec((tr, C), lambda i: (i, 0)),
        out_shape=jax.ShapeDtypeStruct((C, R) if transpose else (R, C), BF16),
        compiler_params=_cparams(("parallel",)),
        name="cast_t" if transpose else "cast",
    )(x)


def _proj_kernel(h_ref, w_ref, cos_ref, sin_ref, o_ref, *, rope, scale):
    acc = _dot(h_ref[...], w_ref[...])
    if scale != 1.0:
        acc = acc * scale
    if rope:
        cos = cos_ref[...]
        sin = sin_ref[...]
        for c in range(acc.shape[1] // HEAD_DIM):
            a = acc[:, c * HEAD_DIM:(c + 1) * HEAD_DIM]
            r = a * cos + pltpu.roll(a, HEAD_DIM // 2, 1) * sin
            o_ref[:, c * HEAD_DIM:(c + 1) * HEAD_DIM] = r.astype(o_ref.dtype)
    else:
        o_ref[...] = acc.astype(o_ref.dtype)


def project(h, w, cos, sin, *, rope, out_dtype, tm, tn, scale=1.0):
    M, K = h.shape
    N = w.shape[1]
    tn = min(tn, N)
    return pl.pallas_call(
        functools.partial(_proj_kernel, rope=rope, scale=scale),
        grid=(M // tm, N // tn),
        in_specs=[pl.BlockSpec((tm, K), lambda i, j: (i, 0)),
                  pl.BlockSpec((K, tn), lambda i, j: (0, j)),
                  pl.BlockSpec((tm, HEAD_DIM), lambda i, j: (i, 0)),
                  pl.BlockSpec((tm, HEAD_DIM), lambda i, j: (i, 0))],
        out_specs=pl.BlockSpec((tm, tn), lambda i, j: (i, j)),
        out_shape=jax.ShapeDtypeStruct((M, N), out_dtype),
        compiler_params=_cparams(("parallel", "arbitrary")),
        name="project_rope" if rope else "project",
    )(h, w, cos, sin)


def rope_tables(pos):
    half = HEAD_DIM // 2
    inv = ROPE_THETA ** (-jnp.arange(half, dtype=F32) / half)
    ang = pos.astype(F32)[:, None] * inv[None, :]
    cos = jnp.cos(ang)
    sin = jnp.sin(ang)
    return jnp.concatenate([cos, cos], axis=-1), jnp.concatenate([-sin, sin], axis=-1)


def _moba_prompt_kernel(q_ref, k_ref, v_ref, o_ref, kb_scr, vt_scr, mean_scr, sel_scr, *, nb):
    i = pl.program_id(2)
    blk = MOBA_BLOCK
    g = q_ref.shape[1] // HEAD_DIM
    nq = g * blk

    @pl.when(i == 0)
    def _():
        for j in range(nb):
            kj = k_ref[j * blk:(j + 1) * blk, :]
            mean_scr[j:j + 1, :] = jnp.sum(kj, axis=0, keepdims=True) / blk
            kb_scr[j] = kj.astype(BF16)
            vt_scr[j] = v_ref[j * blk:(j + 1) * blk, :].T.astype(BF16)

    q = q_ref[...]
    qs = jnp.concatenate([q[:, h * HEAD_DIM:(h + 1) * HEAD_DIM] for h in range(g)], axis=0)

    gate = _dot_nt(mean_scr[...], qs.astype(F32), precision=lax.Precision.HIGHEST)
    row = lax.broadcasted_iota(jnp.int32, (nb, nq), 0)
    valid = row < i
    gate = jnp.where(valid, gate, NEG_INF)
    rank = jnp.zeros((nb, nq), jnp.int32)
    for jp in range(nb):
        gj = gate[jp:jp + 1, :]
        beats = (gj > gate) | ((gj == gate) & (jp < row))
        rank = rank + beats.astype(jnp.int32)
    sel_scr[...] = jnp.where(valid & (rank < MOBA_TOPK), 1.0, 0.0).astype(F32)

    scale = HEAD_DIM ** -0.5
    s = _dot_nt(kb_scr[i], qs) * scale
    kpos = lax.broadcasted_iota(jnp.int32, (blk, nq), 0)
    qpos = lax.broadcasted_iota(jnp.int32, (blk, nq), 1) % blk
    s = jnp.where(kpos <= qpos, s, NEG_INF)
    m0 = jnp.max(s, axis=0, keepdims=True)
    p = jnp.exp(s - m0)
    l0 = jnp.sum(p, axis=0, keepdims=True)
    acc0 = _dot(vt_scr[i], p.astype(BF16))

    def body(j, carry):
        m, l, acc = carry
        sj = _dot_nt(kb_scr[j], qs) * scale
        selj = sel_scr[pl.ds(j, 1), :] > 0.5
        sj = jnp.where(selj, sj, NEG_INF)
        m_new = jnp.maximum(m, jnp.max(sj, axis=0, keepdims=True))
        pj = jnp.exp(sj - m_new)
        alpha = jnp.exp(m - m_new)
        l = alpha * l + jnp.sum(pj, axis=0, keepdims=True)
        acc = alpha * acc + _dot(vt_scr[j], pj.astype(BF16))
        return m_new, l, acc

    _, l, acc = lax.fori_loop(0, i, body, (m0, l0, acc0))
    out = acc / l
    for h in range(g):
        o_ref[:, h * HEAD_DIM:(h + 1) * HEAD_DIM] = out[:, h * blk:(h + 1) * blk].T.astype(o_ref.dtype)


def moba_prompt(q, k, v, B, T, H, q_col0=0):
    KVH = k.shape[1] // HEAD_DIM
    g = H // KVH
    nb = T // MOBA_BLOCK
    qb0 = q_col0 // (g * HEAD_DIM)
    return pl.pallas_call(
        functools.partial(_moba_prompt_kernel, nb=nb),
        grid=(B, KVH, nb),
        in_specs=[pl.BlockSpec((MOBA_BLOCK, g * HEAD_DIM), lambda b, k_, i: (b * nb + i, qb0 + k_)),
                  pl.BlockSpec((T, HEAD_DIM), lambda b, k_, i: (b, k_)),
                  pl.BlockSpec((T, HEAD_DIM), lambda b, k_, i: (b, k_))],
        out_specs=pl.BlockSpec((MOBA_BLOCK, g * HEAD_DIM), lambda b, k_, i: (b * nb + i, k_)),
        out_shape=jax.ShapeDtypeStruct((B * T, H * HEAD_DIM), BF16),
        scratch_shapes=[pltpu.VMEM((nb, MOBA_BLOCK, HEAD_DIM), BF16),
                        pltpu.VMEM((nb, HEAD_DIM, MOBA_BLOCK), BF16),
                        pltpu.VMEM((nb, HEAD_DIM), F32),
                        pltpu.VMEM((nb, g * MOBA_BLOCK), F32)],
        compiler_params=_cparams(("parallel", "parallel", "arbitrary")),
        name="moba_prompt",
    )(q, k, v)


def _ordered_key(x):
    u = pltpu.bitcast(x, jnp.int32)
    return u ^ ((u >> 31) & jnp.int32(0x7FFFFFFF))


def _dsa_prompt_kernel(qi_ref, wi_ref, q_ref, ki_ref, k_ref, v_ref, o_ref,
                       kib_scr, kb_scr, vt_scr, key_scr, sel_scr, *, nb, k_sel):
    i = pl.program_id(1)
    tq = qi_ref.shape[0]
    kvh_n = k_ref.shape[1] // HEAD_DIM
    h_n = q_ref.shape[1] // HEAD_DIM
    g = h_n // kvh_n
    hi_n = qi_ref.shape[1] // HEAD_DIM
    int_min = jnp.int32(-2 ** 31)

    @pl.when(i == 0)
    def _():
        for j in range(nb):
            rows = slice(j * tq, (j + 1) * tq)
            kib_scr[j] = ki_ref[rows, :].astype(BF16)
            for c in range(kvh_n):
                cols = slice(c * HEAD_DIM, (c + 1) * HEAD_DIM)
                kb_scr[c * nb + j] = k_ref[rows, cols].astype(BF16)
                vt_scr[c * nb + j] = v_ref[rows, cols].T.astype(BF16)

    wit = wi_ref[...].T * (hi_n * HEAD_DIM) ** -0.5
    qi = qi_ref[...]

    kpos = lax.broadcasted_iota(jnp.int32, (tq, tq), 0)
    qpos = lax.broadcasted_iota(jnp.int32, (tq, tq), 1)
    causal = kpos <= qpos

    def score_blk(j, _):
        kib = kib_scr[j]
        acc = jnp.zeros((tq, tq), F32)
        for h in range(hi_n):
            r = _dot_nt(kib, qi[:, h * HEAD_DIM:(h + 1) * HEAD_DIM])
            acc = acc + jnp.maximum(r, 0.0) * wit[h:h + 1, :]
        key = _ordered_key(acc)
        key = jnp.where((j < i) | causal, key, int_min)
        key_scr[j] = key
        return 0

    lax.fori_loop(0, i + 1, score_blk, 0)

    def count(pred_fn):
        def blk(j, c):
            m = pred_fn(key_scr[j], j)
            return c + jnp.sum(m.astype(jnp.int32).reshape(tq // 8, 8, tq), axis=0)
        c8 = lax.fori_loop(0, i + 1, blk, jnp.zeros((8, tq), jnp.int32))
        return jnp.sum(c8, axis=0, keepdims=True)

    def bit_step(it, t_u):
        bit = lax.shift_left(jnp.int32(1), 31 - it)
        cand_u = t_u | bit
        cand = cand_u ^ int_min
        c = count(lambda kk, j: kk >= cand)
        return jnp.where(c >= k_sel, cand_u, t_u)

    t_u = lax.fori_loop(0, 32, bit_step, jnp.zeros((1, tq), jnp.int32))
    thr = t_u ^ int_min
    need = k_sel - count(lambda kk, j: kk > thr)

    def idx_of(j):
        return lax.broadcasted_iota(jnp.int32, (tq, tq), 0) + j * tq

    n_bits = max(1, int(math.ceil(math.log2(nb * tq))))

    def tie_step(it, y):
        cand = y | lax.shift_left(jnp.int32(1), n_bits - 1 - it)
        c = count(lambda kk, j: (kk == thr) & (idx_of(j) < cand))
        return jnp.where(c < need, cand, y)

    surplus = jnp.max(count(lambda kk, j: kk == thr) - need)
    y = lax.cond(surplus > 0,
                 lambda: lax.fori_loop(0, n_bits, tie_step, jnp.zeros((1, tq), jnp.int32)),
                 lambda: jnp.full((1, tq), 2 ** n_bits - 1, jnp.int32))

    def sel_blk(j, _):
        kk = key_scr[j]
        s = (kk > thr) | ((kk == thr) & (idx_of(j) <= y))
        s = s & (kk > int_min)
        sel_scr[j] = jnp.where(s, 1.0, 0.0).astype(F32)
        return 0

    lax.fori_loop(0, i + 1, sel_blk, 0)

    scale = HEAD_DIM ** -0.5
    q = q_ref[...]
    for c in range(kvh_n):
        qs = jnp.concatenate([q[:, h * HEAD_DIM:(h + 1) * HEAD_DIM] for h in range(c * g, (c + 1) * g)], axis=0)

        def body(j, carry, c=c, qs=qs):
            m, l, acc = carry
            s = _dot_nt(kb_scr[c * nb + j], qs) * scale
            sel = jnp.concatenate([sel_scr[j]] * g, axis=1) > 0.5
            m_new = jnp.maximum(m, jnp.max(jnp.where(sel, s, -1e30), axis=0, keepdims=True))
            p = jnp.where(sel, jnp.exp(s - m_new), 0.0)
            alpha = jnp.exp(m - m_new)
            l = alpha * l + jnp.sum(p, axis=0, keepdims=True)
            acc = alpha * acc + _dot(vt_scr[c * nb + j], p.astype(BF16))
            return m_new, l, acc

        init = (jnp.full((1, g * tq), -1e30, F32), jnp.zeros((1, g * tq), F32),
                jnp.zeros((HEAD_DIM, g * tq), F32))
        _, l, acc = lax.fori_loop(0, i + 1, body, init)
        out = acc / l
        for hh in range(g):
            h = c * g + hh
            o_ref[:, h * HEAD_DIM:(h + 1) * HEAD_DIM] = out[:, hh * tq:(hh + 1) * tq].T.astype(o_ref.dtype)


def dsa_prompt(qi, wi, q, ki, k, v, B, T, H, qi_col0=0, q_col0=0):
    tq = 256
    nb = T // tq
    k_sel = min(DSA_TOPK, T // 4)
    kvh_n = k.shape[1] // HEAD_DIM
    wqi = IDX_HEADS * HEAD_DIM
    wq = H * HEAD_DIM
    qib, qb = qi_col0 // wqi, q_col0 // wq
    return pl.pallas_call(
        functools.partial(_dsa_prompt_kernel, nb=nb, k_sel=k_sel),
        grid=(B, nb),
        in_specs=[pl.BlockSpec((tq, wqi), lambda b, i: (b * nb + i, qib)),
                  pl.BlockSpec((tq, LANES), lambda b, i: (b * nb + i, 0)),
                  pl.BlockSpec((tq, wq), lambda b, i: (b * nb + i, qb)),
                  pl.BlockSpec((T, HEAD_DIM), lambda b, i: (b, 0)),
                  pl.BlockSpec((T, k.shape[1]), lambda b, i: (b, 0)),
                  pl.BlockSpec((T, v.shape[1]), lambda b, i: (b, 0))],
        out_specs=pl.BlockSpec((tq, wq), lambda b, i: (b * nb + i, 0)),
        out_shape=jax.ShapeDtypeStruct((B * T, wq), BF16),
        scratch_shapes=[pltpu.VMEM((nb, tq, HEAD_DIM), BF16),
                        pltpu.VMEM((kvh_n * nb, tq, HEAD_DIM), BF16),
                        pltpu.VMEM((kvh_n * nb, HEAD_DIM, tq), BF16),
                        pltpu.VMEM((nb, tq, tq), jnp.int32),
                        pltpu.VMEM((nb, tq, tq), F32)],
        compiler_params=_cparams(("parallel", "arbitrary")),
        name="dsa_prompt",
    )(qi, wi, q, ki, k, v)


def _resident(shape):
    nd = len(shape)
    return pl.BlockSpec(shape, lambda *_: (0,) * nd, pipeline_mode=pl.Buffered(1))


def _merge_kernel(oa_ref, ob_ref, ga_ref, gb_ref, x_ref, wa_ref, wb_ref, wo_ref, o_ref):
    ya = _dot(oa_ref[...], wa_ref[...])
    yb = _dot(ob_ref[...], wb_ref[...])
    z = jax.nn.sigmoid(ga_ref[...].astype(F32)) * ya + jax.nn.sigmoid(gb_ref[...].astype(F32)) * yb
    o_ref[...] = x_ref[...] + _dot(z.astype(BF16), wo_ref[...])


def merge(o_a, o_b, gates, x, w_a, w_b, w_o, tm):
    M, D = x.shape
    return pl.pallas_call(
        _merge_kernel,
        grid=(M // tm,),
        in_specs=[pl.BlockSpec((tm, o_a.shape[1]), lambda i: (i, 0)),
                  pl.BlockSpec((tm, o_b.shape[1]), lambda i: (i, 0)),
                  pl.BlockSpec((tm, D), lambda i: (i, 0)),
                  pl.BlockSpec((tm, D), lambda i: (i, 1)),
                  pl.BlockSpec((tm, D), lambda i: (i, 0)),
                  _resident(w_a.shape), _resident(w_b.shape), _resident(w_o.shape)],
        out_specs=pl.BlockSpec((tm, D), lambda i: (i, 0)),
        out_shape=jax.ShapeDtypeStruct((M, D), F32),
        compiler_params=_cparams(("parallel",)),
        name="merge",
    )(o_a, o_b, gates, gates, x, w_a, w_b, w_o)


def _top_values(x, n, order=None, exact=True):
    vals = []
    if exact:
        if order is None:
            order = lax.broadcasted_iota(jnp.int32, x.shape, 0)
        big = jnp.int32(2 ** 30)
        for _ in range(n):
            m = jnp.max(x, axis=0, keepdims=True)
            vals.append(m)
            first = jnp.min(jnp.where(x == m, order, big), axis=0, keepdims=True)
            x = jnp.where(order == first, NEG_INF, x)
        return jnp.concatenate(vals, axis=0), jnp.zeros((1, x.shape[1]), jnp.int32)
    y = x
    for _ in range(n):
        m = jnp.max(y, axis=0, keepdims=True)
        vals.append(m)
        y = jnp.where(y == m, NEG_INF, y)
    removed = (jnp.sum(jnp.where(y == NEG_INF, 1, 0), axis=0, keepdims=True)
               - jnp.sum(jnp.where(x == NEG_INF, 1, 0), axis=0, keepdims=True))
    return jnp.concatenate(vals, axis=0), jnp.abs(removed - n)


def _pair_candidates(v1, v2):
    k = PEER_TOPK
    tm = v1.shape[1]
    sub = lax.broadcasted_iota(jnp.int32, (8, tm), 0)
    vals = [v1[0:1, :] + v2]
    order = [lax.broadcasted_iota(jnp.int32, (k, tm), 0)]
    for a in range(1, 8):
        vals.append(jnp.where(sub < k // (a + 1), v1[a:a + 1, :] + v2[0:8, :], NEG_INF))
        order.append(sub + a * k)
    vals.append(v1[8:16, :] + v2[0:1, :])
    order.append((sub + 8) * k)
    return jnp.concatenate(vals, axis=0), jnp.concatenate(order, axis=0)


def _peer_topk_kernel(x_ref, g_ref, wq_ref, k1_ref, k2_ref, ht_ref, s1_ref, s2_ref, st_ref, q_scr):
    tm = x_ref.shape[0]
    hn = x_ref[...]
    hn = _rms(hn, g_ref[...])
    ht_ref[...] = hn.T.astype(BF16)
    q = _dot(hn.astype(BF16), wq_ref[...]).astype(BF16)
    kd = 2 * PEER_N_KEYS
    for hd in range(PEER_HEADS):
        q_scr[hd] = q[:, hd * kd:(hd + 1) * kd]

    def head(hd, _):
        qh = q_scr[hd]
        s1 = _dot_nt(k1_ref[hd], qh[:, :PEER_N_KEYS])
        s2 = _dot_nt(k2_ref[hd], qh[:, PEER_N_KEYS:])
        s1_ref[hd] = s1
        s2_ref[hd] = s2

        def stats(exact):
            out, bad = [], jnp.zeros((1, LANES), jnp.int32)
            for tc in range(tm // LANES):
                cols = slice(tc * LANES, (tc + 1) * LANES)
                v1, b1 = _top_values(s1[:, cols], PEER_TOPK, exact=exact)
                v2, b2 = _top_values(s2[:, cols], PEER_TOPK, exact=exact)
                cand, order = _pair_candidates(v1, v2)
                best, b3 = _top_values(cand, PEER_TOPK, order, exact=exact)
                z = jnp.sum(jnp.exp(best - best[0:1, :]), axis=0, keepdims=True)
                out.append(jnp.concatenate([best[PEER_TOPK - 1:PEER_TOPK, :], v1[0:1, :], v2[0:1, :], 1.0 / z,
                                            jnp.zeros((4, LANES), F32)], axis=0))
                bad = bad + b1 + b2 + b3
            return jnp.concatenate(out, axis=1), bad

        st, bad = stats(False)
        st_ref[hd] = lax.cond(jnp.max(bad) > 0, lambda: stats(True)[0], lambda: st)
        return 0

    lax.fori_loop(0, PEER_HEADS, head, 0)


def peer_topk(x, g, w_q, keys1, keys2, tm):
    M, D = x.shape
    hp = PEER_HEADS
    return pl.pallas_call(
        _peer_topk_kernel,
        grid=(M // tm,),
        in_specs=[pl.BlockSpec((tm, D), lambda i: (i, 0)),
                  pl.BlockSpec((1, D), lambda i: (0, 0)),
                  _resident(w_q.shape), _resident(keys1.shape), _resident(keys2.shape)],
        out_specs=[pl.BlockSpec((D, tm), lambda i: (0, i)),
                   pl.BlockSpec((hp, PEER_N_KEYS, tm), lambda i: (0, 0, i)),
                   pl.BlockSpec((hp, PEER_N_KEYS, tm), lambda i: (0, 0, i)),
                   pl.BlockSpec((hp, 8, tm), lambda i: (0, 0, i))],
        out_shape=[jax.ShapeDtypeStruct((D, M), BF16),
                   jax.ShapeDtypeStruct((hp, PEER_N_KEYS, M), F32),
                   jax.ShapeDtypeStruct((hp, PEER_N_KEYS, M), F32),
                   jax.ShapeDtypeStruct((hp, 8, M), F32)],
        scratch_shapes=[pltpu.VMEM((hp, tm, 2 * PEER_N_KEYS), BF16)],
        compiler_params=_cparams(("parallel",)),
        name="peer_topk",
    )(x, g.reshape(1, D), w_q, keys1, keys2)


def _peer_dense_kernel(ht_ref, s1_ref, s2_ref, st_ref, u_ref, vt_ref, o_ref, e1_scr, e2_scr):
    e = pl.program_id(1)
    n_i1 = u_ref.shape[0] // PEER_N_KEYS

    @pl.when(e == 0)
    def _():
        o_ref[...] = jnp.zeros_like(o_ref)
        for hd in range(PEER_HEADS):
            st = st_ref[hd]
            e1_scr[hd] = jnp.exp(s1_ref[hd] - st[1:2, :]) * st[3:4, :]
            e2_scr[hd] = jnp.exp(s2_ref[hd] - st[2:3, :])

    a = _dot(u_ref[...], ht_ref[...])
    act = 0.5 * a * (1.0 + lax.erf(a * (2.0 ** -0.5)))
    zs = []
    for r in range(n_i1):
        i1 = e * n_i1 + r
        w = jnp.zeros((PEER_N_KEYS, a.shape[1]), F32)
        for hd in range(PEER_HEADS):
            t = s1_ref[hd, pl.ds(i1, 1), :] + s2_ref[hd]
            w = w + jnp.where(t >= st_ref[hd, 0:1, :], e1_scr[hd, pl.ds(i1, 1), :] * e2_scr[hd], 0.0)
        zs.append((act[r * PEER_N_KEYS:(r + 1) * PEER_N_KEYS, :] * w).astype(BF16))
    z = jnp.concatenate(zs, axis=0)
    o_ref[...] += _dot(vt_ref[...], z)


def peer_dense(ht, s1t, s2t, stats, u, vt, tm, e_blk):
    D, M = ht.shape
    E = u.shape[0]
    hp = PEER_HEADS
    return pl.pallas_call(
        _peer_dense_kernel,
        grid=(M // tm, E // e_blk),
        in_specs=[pl.BlockSpec((D, tm), lambda i, e: (0, i)),
                  pl.BlockSpec((hp, PEER_N_KEYS, tm), lambda i, e: (0, 0, i)),
                  pl.BlockSpec((hp, PEER_N_KEYS, tm), lambda i, e: (0, 0, i)),
                  pl.BlockSpec((hp, 8, tm), lambda i, e: (0, 0, i)),
                  pl.BlockSpec((e_blk, D), lambda i, e: (e, 0)),
                  pl.BlockSpec((D, e_blk), lambda i, e: (0, e))],
        out_specs=pl.BlockSpec((D, tm), lambda i, e: (0, i)),
        out_shape=jax.ShapeDtypeStruct((D, M), F32),
        scratch_shapes=[pltpu.VMEM((hp, PEER_N_KEYS, tm), F32),
                        pltpu.VMEM((hp, PEER_N_KEYS, tm), F32)],
        compiler_params=_cparams(("parallel", "arbitrary")),
        name="peer_dense",
    )(ht, s1t, s2t, stats, u, vt)


def _ple_kernel(x_ref, pt_ref, p_ref, g_ref, wg_ref, wp_ref, gf_ref, o_ref, *, final_norm):
    x2 = x_ref[...] + pt_ref[...].T
    h = _rms(x2, g_ref[...]).astype(BF16)
    gate = jax.nn.sigmoid(_dot(h, wg_ref[...]))
    x3 = x2 + gate * _dot(p_ref[...].astype(BF16), wp_ref[...])
    if final_norm:
        x3 = _rms(x3, gf_ref[...])
    o_ref[...] = x3


def ple(x, peer_t, p, g, w_gate, w_proj, g_final, tm, final_norm):
    M, D = x.shape
    return pl.pallas_call(
        functools.partial(_ple_kernel, final_norm=final_norm),
        grid=(M // tm,),
        in_specs=[pl.BlockSpec((tm, D), lambda i: (i, 0)),
                  pl.BlockSpec((D, tm), lambda i: (0, i)),
                  pl.BlockSpec((tm, p.shape[1]), lambda i: (i, 0)),
                  pl.BlockSpec((1, D), lambda i: (0, 0)),
                  _resident(w_gate.shape), _resident(w_proj.shape),
                  pl.BlockSpec((1, D), lambda i: (0, 0))],
        out_specs=pl.BlockSpec((tm, D), lambda i: (i, 0)),
        out_shape=jax.ShapeDtypeStruct((M, D), F32),
        compiler_params=_cparams(("parallel",)),
        name="ple",
    )(x, peer_t, p, g.reshape(1, D), w_gate, w_proj, g_final.reshape(1, D))


SAMPLE_PPS = 8
SCAN_SLOTS = 4
NBP_PAD = 128


def _sample_scan_kernel(pt_ref, qi_ref, wi_ref, kin_ref, qa_ref, cak_ref, cki_ref, isc_ref, inew_ref, ids_ref,
                        ak_buf, ki_buf, sums_scr, sem, *, layer, pps, ps, nbp, n_chunks):
    b = pl.program_id(0)
    n_seq = pl.num_programs(0)
    kvh_n = sums_scr.shape[0]
    ppb = MOBA_BLOCK // ps
    qi = qi_ref[...]
    w = wi_ref[...]

    def copies(seq, ch, slot):
        out = []
        for r in range(pps):
            phys = pt_ref[seq, ch * pps + r]
            out.append(pltpu.make_async_copy(cak_ref.at[layer, phys], ak_buf.at[slot, r], sem.at[0, slot]))
            out.append(pltpu.make_async_copy(cki_ref.at[layer, phys], ki_buf.at[slot, r], sem.at[1, slot]))
        return out

    ahead = SCAN_SLOTS - 1

    @pl.when(b == 0)
    def _():
        for ch0 in range(ahead):
            for c in copies(0, ch0, ch0):
                c.start()

    sums_scr[...] = jnp.zeros_like(sums_scr)
    kin = kin_ref[...].astype(BF16).astype(F32)
    r_new = jnp.sum(qi.astype(F32) * kin, axis=1, keepdims=True)
    inew_ref[...] = jnp.broadcast_to(jnp.sum(jnp.maximum(r_new, 0.0) * w, axis=0, keepdims=True), inew_ref.shape)

    def chunk(ch, _):
        slot = lax.rem(ch, SCAN_SLOTS)
        nxt = ch + ahead
        nxt_slot = lax.rem(nxt, SCAN_SLOTS)

        @pl.when(nxt < n_chunks)
        def _():
            for c in copies(b, nxt, nxt_slot):
                c.start()

        @pl.when((nxt >= n_chunks) & (b + 1 < n_seq))
        def _():
            for c in copies(b + 1, nxt - n_chunks, nxt_slot):
                c.start()

        for c in copies(b, ch, slot):
            c.wait()
        for r in range(pps):
            x = ak_buf[slot, r]
            n = ps
            while n > 1:
                n //= 2
                x = x[:n] + x[n:]
            blk = lax.div(ch * pps + r, ppb)
            for c in range(kvh_n):
                sums_scr[c, pl.ds(blk, 1), :] += x[0, c:c + 1, :]
            sc = _dot_nt(qi, ki_buf[slot, r].astype(BF16))
            isc_ref[pl.ds(ch * pps + r, 1), :] = jnp.sum(jnp.maximum(sc, 0.0) * w, axis=0, keepdims=True)
        return 0

    lax.fori_loop(0, n_chunks, chunk, 0)

    qa = qa_ref[...].astype(F32)
    h_n = qa.shape[0]
    g = h_n // kvh_n
    row = lax.broadcasted_iota(jnp.int32, (h_n, NBP_PAD), 0)
    lane = lax.broadcasted_iota(jnp.int32, (h_n, NBP_PAD), 1)
    gate = jnp.full((h_n, NBP_PAD), NEG_INF, F32)
    for c in range(kvh_n):
        gc = _dot_nt(qa, sums_scr[c] / MOBA_BLOCK, precision=lax.Precision.HIGHEST)
        gate = jnp.where((row >= c * g) & (row < (c + 1) * g), gc, gate)
    gate = jnp.where(lane < nbp, gate, NEG_INF)
    ids = jnp.zeros((h_n, NBP_PAD), jnp.int32)
    for r in range(MOBA_TOPK):
        m = jnp.max(gate, axis=1, keepdims=True)
        first = jnp.min(jnp.where(gate == m, lane, NBP_PAD), axis=1, keepdims=True)
        ids = jnp.where(lane == r, first, ids)
        gate = jnp.where(lane == first, NEG_INF, gate)
    ids_ref[...] = ids


def sample_scan(page_table, cache_a_k, cache_b_kidx, layer, qi, wi, ki_new, qa):
    bs, n_pages = page_table.shape
    ps, kvh_n = cache_a_k.shape[2], cache_a_k.shape[3]
    pps = SAMPLE_PPS
    nbp = n_pages * ps // MOBA_BLOCK
    h_n, hi_n = qa.shape[1], qi.shape[1]
    per_seq = lambda shape: pl.BlockSpec((None,) + shape, lambda b, pt: (b,) + (0,) * len(shape))
    hbm = pl.BlockSpec(memory_space=pl.ANY)
    grid_spec = pltpu.PrefetchScalarGridSpec(
        num_scalar_prefetch=1,
        grid=(bs,),
        in_specs=[per_seq((hi_n, HEAD_DIM)), per_seq((hi_n, 1)), per_seq((1, HEAD_DIM)), per_seq((h_n, HEAD_DIM)),
                  hbm, hbm],
        out_specs=[per_seq((n_pages, ps)), per_seq((1, LANES)), per_seq((h_n, NBP_PAD))],
        scratch_shapes=[pltpu.VMEM((SCAN_SLOTS, pps, ps, kvh_n, HEAD_DIM), F32),
                        pltpu.VMEM((SCAN_SLOTS, pps, ps, HEAD_DIM), F32),
                        pltpu.VMEM((kvh_n, NBP_PAD, HEAD_DIM), F32),
                        pltpu.SemaphoreType.DMA((2, SCAN_SLOTS))],
    )
    return pl.pallas_call(
        functools.partial(_sample_scan_kernel, layer=layer, pps=pps, ps=ps, nbp=nbp, n_chunks=n_pages // pps),
        grid_spec=grid_spec,
        out_shape=[jax.ShapeDtypeStruct((bs, n_pages, ps), F32),
                   jax.ShapeDtypeStruct((bs, 1, LANES), F32),
                   jax.ShapeDtypeStruct((bs, h_n, NBP_PAD), jnp.int32)],
        compiler_params=_cparams(("arbitrary",)),
        name="sample_scan",
    )(page_table, qi, wi, ki_new, qa, cache_a_k, cache_b_kidx)


def _sample_select_kernel(isc_ref, inew_ref, pos_ref, sel_scr, new_scr, rank_scr, *, k_sel, past):
    bs, n_pages, ps = isc_ref.shape
    int_min = jnp.int32(-2 ** 31)
    keys = _ordered_key(isc_ref[...])
    knew = _ordered_key(inew_ref[...][:, :, 0:1])
    idx = (lax.broadcasted_iota(jnp.int32, keys.shape, 1) * ps
           + lax.broadcasted_iota(jnp.int32, keys.shape, 2))

    def count(m_cache, m_new):
        c = jnp.sum(m_cache.astype(jnp.int32), axis=1, keepdims=True)
        return jnp.sum(c, axis=2, keepdims=True) + m_new.astype(jnp.int32)

    def bit_step(it, t_u):
        cand_u = t_u | lax.shift_left(jnp.int32(1), 31 - it)
        cand = cand_u ^ int_min
        return jnp.where(count(keys >= cand, knew >= cand) >= k_sel, cand_u, t_u)

    thr = lax.fori_loop(0, 32, bit_step, jnp.zeros((bs, 1, 1), jnp.int32)) ^ int_min
    need = k_sel - count(keys > thr, knew > thr)
    n_bits = int(math.ceil(math.log2(past + 1)))

    def tie_step(it, y):
        cand = y | lax.shift_left(jnp.int32(1), n_bits - 1 - it)
        c = count((keys == thr) & (idx < cand), (knew == thr) & (past < cand))
        return jnp.where(c < need, cand, y)

    y = lax.fori_loop(0, n_bits, tie_step, jnp.zeros((bs, 1, 1), jnp.int32))
    sel_scr[...] = jnp.where((keys > thr) | ((keys == thr) & (idx <= y)), 1.0, 0.0).astype(F32)
    new_sel = (knew > thr) | ((knew == thr) & (past <= y))
    new_scr[...] = jnp.broadcast_to(jnp.where(new_sel, 1.0, 0.0).astype(F32), new_scr.shape)

    upper = (lax.broadcasted_iota(jnp.int32, (ps, ps), 0) < lax.broadcasted_iota(jnp.int32, (ps, ps), 1))
    lower = (lax.broadcasted_iota(jnp.int32, (n_pages, n_pages), 1)
             < lax.broadcasted_iota(jnp.int32, (n_pages, n_pages), 0))
    upper = jnp.where(upper, 1.0, 0.0).astype(BF16)
    lower = jnp.where(lower, 1.0, 0.0).astype(BF16)
    r_iota = lax.broadcasted_iota(jnp.int32, (k_sel, ps), 0).astype(F32)
    slot = lax.broadcasted_iota(jnp.int32, (k_sel, ps), 1).astype(F32)
    lane = lax.broadcasted_iota(jnp.int32, (1, k_sel), 1)

    def per_seq(b, _):
        sb = sel_scr[b]
        prefix = _dot(sb.astype(BF16), upper)
        cnt = jnp.sum(sb, axis=1, keepdims=True)
        offs = _dot(lower, jnp.broadcast_to(cnt, sb.shape).astype(BF16))
        rank_scr[...] = offs + prefix

        def per_page(p, acc):
            hit = (r_iota == rank_scr[pl.ds(p, 1), :]) & (sel_scr[b, pl.ds(p, 1), :] > 0.5)
            return acc + jnp.where(hit, lax.convert_element_type(p * ps, F32) + slot, 0.0)

        acc = lax.fori_loop(0, n_pages, per_page, jnp.zeros((k_sel, ps), F32))
        tot = jnp.sum(acc.T, axis=0, keepdims=True)
        tot = tot + jnp.where(lane == k_sel - 1, new_scr[b][:, 0:1] * past, 0.0)
        pos_ref[b] = tot.astype(jnp.int32)
        return 0

    lax.fori_loop(0, bs, per_seq, 0)


def sample_select(iscore, inew, k_sel, past):
    bs, n_pages, ps = iscore.shape
    return pl.pallas_call(
        functools.partial(_sample_select_kernel, k_sel=k_sel, past=past),
        out_shape=jax.ShapeDtypeStruct((bs, 1, k_sel), jnp.int32),
        scratch_shapes=[pltpu.VMEM((bs, n_pages, ps), F32),
                        pltpu.VMEM((bs, 1, LANES), F32),
                        pltpu.VMEM((n_pages, ps), F32)],
        compiler_params=pltpu.CompilerParams(vmem_limit_bytes=VMEM_LIMIT),
        name="sample_select",
    )(iscore, inew)


def _sample_attend_kernel(pt_ref, ids_ref, pos_ref, qa_ref, kan_ref, van_ref, qb_ref, kbn_ref, vbn_ref,
                          cak_ref, cav_ref, cbk_ref, cbv_ref, oa_ref, ob_ref,
                          ka_buf, va_buf, kb_buf, vb_buf, sem, *, layer, ps, past, k_sel):
    b = pl.program_id(0)
    n_seq = pl.num_programs(0)
    h_n = qa_ref.shape[0]
    kvh_n = kan_ref.shape[0]
    g = h_n // kvh_n
    ppb = MOBA_BLOCK // ps
    scale = HEAD_DIM ** -0.5
    slot = lax.rem(b, 2)

    def moba_copies(seq, sl, h, r, half):
        phys = pt_ref[seq, ids_ref[seq, h * MOBA_TOPK + r] * ppb + half]
        rows = pl.ds((r * ppb + half) * ps, ps)
        return (pltpu.make_async_copy(cak_ref.at[layer, phys, :, h // g, :], ka_buf.at[sl, h, rows, :], sem.at[0, sl]),
                pltpu.make_async_copy(cav_ref.at[layer, phys, :, h // g, :], va_buf.at[sl, h, rows, :], sem.at[1, sl]))

    def dsa_copies(seq, sl, j):
        s = jnp.minimum(pos_ref[seq, j], past - 1)
        phys = pt_ref[seq, lax.div(s, ps)]
        row = lax.rem(s, ps)
        return (pltpu.make_async_copy(cbk_ref.at[layer, phys, row], kb_buf.at[sl, :, j, :], sem.at[2, sl]),
                pltpu.make_async_copy(cbv_ref.at[layer, phys, row], vb_buf.at[sl, :, j, :], sem.at[3, sl]))

    moba_keys = [(h, r, half) for h in range(h_n) for r in range(MOBA_TOPK) for half in range(ppb)]

    def transfer(seq, sl, start):
        for key in moba_keys:
            for c in moba_copies(seq, sl, *key):
                c.start() if start else c.wait()

        def row(j, _):
            for c in dsa_copies(seq, sl, j):
                c.start() if start else c.wait()
            return 0

        lax.fori_loop(0, k_sel, row, 0)

    @pl.when(b == 0)
    def _():
        transfer(0, 0, True)

    @pl.when(b + 1 < n_seq)
    def _():
        transfer(b + 1, 1 - slot, True)

    transfer(b, slot, False)
    ka_buf, va_buf, kb_buf, vb_buf = ka_buf.at[slot], va_buf.at[slot], kb_buf.at[slot], vb_buf.at[slot]

    @pl.when(pos_ref[b, k_sel - 1] == past)
    def _():
        for c in range(kvh_n):
            kb_buf[c, k_sel - 1:k_sel, :] = kbn_ref[c:c + 1, :]
            vb_buf[c, k_sel - 1:k_sel, :] = vbn_ref[c:c + 1, :]

    qa = qa_ref[...]
    qaf = qa.astype(F32)
    kan = kan_ref[...].astype(BF16).astype(F32)
    van = van_ref[...].astype(BF16).astype(F32)
    for h in range(h_n):
        c = h // g
        s = _dot_nt(qa, ka_buf[h].astype(BF16))[h:h + 1, :] * scale
        s_new = jnp.sum(qaf[h:h + 1, :] * kan[c:c + 1, :], axis=1, keepdims=True) * scale
        m = jnp.maximum(jnp.max(s, axis=1, keepdims=True), s_new)
        p = jnp.exp(s - m)
        p_new = jnp.exp(s_new - m)
        l = jnp.sum(p, axis=1, keepdims=True) + p_new
        pv = _dot(jnp.broadcast_to(p, (8, p.shape[1])).astype(BF16), va_buf[h].astype(BF16))[0:1, :]
        oa_ref[h:h + 1, :] = (pv + p_new * van[c:c + 1, :]) / l

    qb = qb_ref[...]
    for c in range(kvh_n):
        s = _dot_nt(qb, kb_buf[c].astype(BF16)) * scale
        m = jnp.max(s, axis=1, keepdims=True)
        p = jnp.exp(s - m)
        l = jnp.sum(p, axis=1, keepdims=True)
        o = _dot(p.astype(BF16), vb_buf[c].astype(BF16)) / l
        ob_ref[c * g:(c + 1) * g, :] = o[c * g:(c + 1) * g, :]


def sample_attend(page_table, ids, pos, qa, ka_new, va_new, qb, kb_new, vb_new,
                  cache_a_k, cache_a_v, cache_b_k, cache_b_v, layer):
    bs, n_pages = page_table.shape
    ps, kvh_n = cache_a_k.shape[2], cache_a_k.shape[3]
    past = n_pages * ps
    k_sel = pos.shape[1]
    h_n = qa.shape[1]
    per_seq = lambda shape: pl.BlockSpec((None,) + shape, lambda b, *_: (b,) + (0,) * len(shape))
    hbm = pl.BlockSpec(memory_space=pl.ANY)
    grid_spec = pltpu.PrefetchScalarGridSpec(
        num_scalar_prefetch=3,
        grid=(bs,),
        in_specs=[per_seq((h_n, HEAD_DIM)), per_seq((kvh_n, HEAD_DIM)), per_seq((kvh_n, HEAD_DIM)),
                  per_seq((h_n, HEAD_DIM)), per_seq((kvh_n, HEAD_DIM)), per_seq((kvh_n, HEAD_DIM)),
                  hbm, hbm, hbm, hbm],
        out_specs=[per_seq((h_n, HEAD_DIM)), per_seq((h_n, HEAD_DIM))],
        scratch_shapes=[pltpu.VMEM((2, h_n, MOBA_TOPK * MOBA_BLOCK, HEAD_DIM), F32),
                        pltpu.VMEM((2, h_n, MOBA_TOPK * MOBA_BLOCK, HEAD_DIM), F32),
                        pltpu.VMEM((2, kvh_n, k_sel, HEAD_DIM), F32),
                        pltpu.VMEM((2, kvh_n, k_sel, HEAD_DIM), F32),
                        pltpu.SemaphoreType.DMA((4, 2))],
    )
    return pl.pallas_call(
        functools.partial(_sample_attend_kernel, layer=layer, ps=ps, past=past, k_sel=k_sel),
        grid_spec=grid_spec,
        out_shape=[jax.ShapeDtypeStruct((bs, h_n, HEAD_DIM), F32)] * 2,
        compiler_params=_cparams(("arbitrary",)),
        name="sample_attend",
    )(page_table, ids, pos, qa, ka_new, va_new, qb, kb_new, vb_new, cache_a_k, cache_a_v, cache_b_k, cache_b_v)


def _sample_mixers(qa, ka, va, qb, kb, vb, qi, ki, wi, cache_a_k, cache_a_v, cache_b_k, cache_b_v,
                   cache_b_kidx, layer, page_table):
    bs = qa.shape[0]
    ps = cache_a_k.shape[2]
    n_pages = page_table.shape[1]
    past = n_pages * ps
    assert MOBA_BLOCK % ps == 0 and past % MOBA_BLOCK == 0 and n_pages % (SCAN_SLOTS * SAMPLE_PPS) == 0
    assert MOBA_TOPK <= past // MOBA_BLOCK <= NBP_PAD and SAMPLE_PPS % (MOBA_BLOCK // ps) == 0
    k_sel = min(DSA_TOPK, (past + 1) // 4)
    assert k_sel == DSA_TOPK
    r3 = lambda a: a.reshape(bs, -1, HEAD_DIM)
    wi3 = (wi[:, :IDX_HEADS] * (IDX_HEADS * HEAD_DIM) ** -0.5).reshape(bs, IDX_HEADS, 1)
    iscore, inew, ids = sample_scan(page_table, cache_a_k, cache_b_kidx, layer, r3(qi), wi3,
                                    ki.reshape(bs, 1, HEAD_DIM), r3(qa))
    pos = sample_select(iscore, inew, k_sel, past).reshape(bs, k_sel)
    ids = ids[:, :, :MOBA_TOPK].reshape(bs, -1)
    o_a, o_b = sample_attend(page_table, ids, pos, r3(qa), r3(ka), r3(va), r3(qb), r3(kb), r3(vb),
                             cache_a_k, cache_a_v, cache_b_k, cache_b_v, layer)
    return o_a.reshape(bs, -1), o_b.reshape(bs, -1)


N_HEADS_A = 8
N_KV_A = 4
N_HEADS_B = 8
N_KV_B = 4
PLE_PAD_ROWS = 128


def _split_w_in(w):
    d = HEAD_DIM
    sizes = (N_HEADS_A * d, N_KV_A * d, N_KV_A * d, N_HEADS_B * d, N_KV_B * d, N_KV_B * d,
             IDX_HEADS * d, d, IDX_HEADS, w.shape[0], w.shape[0])
    offs = np.concatenate([[0], np.cumsum(sizes)])
    qa, ka, va, qb, kb, vb, qi, ki, wi, ga, gb = (w[:, offs[n]:offs[n + 1]] for n in range(len(sizes)))
    wi = jnp.pad(wi, ((0, 0), (0, LANES - IDX_HEADS)))
    c = lambda *a: jnp.concatenate(a, axis=1).astype(BF16)
    return dict(qq=c(qi, qa, qb), gates=c(ga, gb), ka=c(ka), va=c(va), kb=c(kb), vb=c(vb), ki=c(ki), wi=c(wi))


def _project_all(h, w, cos, sin, tm):
    kw = dict(tm=tm, tn=512)
    return dict(
        qq=project(h, w['qq'], cos, sin, rope=True, out_dtype=BF16, **kw),
        gates=project(h, w['gates'], cos, sin, rope=False, out_dtype=BF16, **kw),
        ka=project(h, w['ka'], cos, sin, rope=True, out_dtype=F32, **kw),
        va=project(h, w['va'], cos, sin, rope=False, out_dtype=F32, **kw),
        kb=project(h, w['kb'], cos, sin, rope=True, out_dtype=F32, **kw),
        vb=project(h, w['vb'], cos, sin, rope=False, out_dtype=F32, **kw),
        ki=project(h, w['ki'], cos, sin, rope=True, out_dtype=F32, **kw),
        wi=project(h, w['wi'], cos, sin, rope=False, out_dtype=F32, **kw),
    )


def _post_mixer(x1, p, lw, g_final, final_norm, tm_topk, tm_dense, tm_ple):
    ht, s1t, s2t, st = peer_topk(x1, lw['g_ffn'], lw['w_peer_q'], lw['keys1'], lw['keys2'], tm_topk)
    pt = peer_dense(ht, s1t, s2t, st, lw['u'], lw['vt'], tm_dense, 1024)
    return ple(x1, pt, p, lw['g_ple'], lw['w_ple_gate'], lw['w_ple_proj'], g_final, tm_ple, final_norm)


def kernel(x_prompt, x_sample, cache_a_k, cache_a_v, cache_b_k, cache_b_v, cache_b_kidx, page_table,
           p_prompt, p_sample, g_attn, w_in, w_branch_a, w_branch_b, w_out, g_ffn, w_peer_q,
           peer_keys1, peer_keys2, peer_u, peer_v, g_ple, w_ple_gate, w_ple_proj, g_final):
    B, T, D = x_prompt.shape
    bs, ts, _ = x_sample.shape
    assert ts == 1
    depth = w_in.shape[0]
    past = page_table.shape[1] * cache_a_k.shape[2]
    mp = B * T
    ms = PLE_PAD_ROWS
    qi_w = IDX_HEADS * HEAD_DIM
    qa_w = N_HEADS_A * HEAD_DIM

    cos_p, sin_p = rope_tables(jnp.arange(T, dtype=jnp.int32))
    cos_p, sin_p = jnp.tile(cos_p, (B, 1)), jnp.tile(sin_p, (B, 1))
    cos_s, sin_s = rope_tables(jnp.full((ms,), past, jnp.int32))

    xp = x_prompt.reshape(mp, D)
    xs = jnp.pad(x_sample.reshape(bs, D), ((0, ms - bs), (0, 0)))
    rows_p = [[] for _ in range(5)]
    rows_s = [[] for _ in range(5)]
    for i in range(depth):
        w = _split_w_in(w_in[i])
        lw = dict(g_ffn=g_ffn[i], w_peer_q=w_peer_q[i].astype(BF16), keys1=peer_keys1[i].astype(BF16),
                  keys2=peer_keys2[i].astype(BF16), u=cast_bf16(peer_u[i], 1024), vt=cast_bf16(peer_v[i], 1024, transpose=True),
                  g_ple=g_ple[i], w_ple_gate=w_ple_gate[i].astype(BF16), w_ple_proj=w_ple_proj[i].astype(BF16))
        w_a, w_b, w_o = w_branch_a[i].astype(BF16), w_branch_b[i].astype(BF16), w_out[i].astype(BF16)
        last = i == depth - 1

        sr = _project_all(rmsnorm_bf16(xs, g_attn[i], ms), w, cos_s, sin_s, ms)
        f = lambda a: a[:bs]
        qq = f(sr['qq'])
        so_a, so_b = _sample_mixers(
            qq[:, qi_w:qi_w + qa_w], f(sr['ka']), f(sr['va']), qq[:, qi_w + qa_w:], f(sr['kb']), f(sr['vb']),
            qq[:, :qi_w], f(sr['ki']), f(sr['wi']),
            cache_a_k, cache_a_v, cache_b_k, cache_b_v, cache_b_kidx, i, page_table)
        padr = lambda a: jnp.pad(a, ((0, ms - bs), (0, 0))).astype(BF16)
        x1s = merge(padr(so_a), padr(so_b), sr['gates'], xs, w_a, w_b, w_o, ms)
        ps = jnp.pad(p_sample[i].reshape(bs, -1), ((0, ms - bs), (0, 0)))
        xs = _post_mixer(x1s, ps, lw, g_final, last, ms, ms, ms)
        for lst, name in zip(rows_s, ('ka', 'va', 'kb', 'vb', 'ki')):
            r = sr[name][:bs]
            lst.append(r.reshape((bs, 1, -1, HEAD_DIM) if name != 'ki' else (bs, 1, HEAD_DIM)))

        pr = _project_all(rmsnorm_bf16(xp, g_attn[i], 512), w, cos_p, sin_p, 1024)
        o_a = moba_prompt(pr['qq'], pr['ka'], pr['va'], B, T, N_HEADS_A, q_col0=qi_w)
        o_b = dsa_prompt(pr['qq'], pr['wi'], pr['qq'], pr['ki'], pr['kb'], pr['vb'], B, T, N_HEADS_B,
                         qi_col0=0, q_col0=qi_w + qa_w)
        x1 = merge(o_a, o_b, pr['gates'], xp, w_a, w_b, w_o, 256)
        xp = _post_mixer(x1, p_prompt[i].reshape(mp, -1), lw, g_final, last, 256, 512, 256)
        for lst, name in zip(rows_p, ('ka', 'va', 'kb', 'vb', 'ki')):
            lst.append(pr[name].reshape((B, T, -1, HEAD_DIM) if name != 'ki' else (B, T, HEAD_DIM)))

    y_prompt = xp.reshape(B, T, D)
    y_sample = xs[:bs].reshape(bs, 1, D)
    return (y_prompt, y_sample) + tuple(jnp.stack(r) for r in rows_p) + tuple(jnp.stack(r) for r in rows_s)
```

```python
import functools
import math

import jax
import jax.numpy as jnp
import numpy as np
from jax import lax
from jax.experimental import pallas as pl
from jax.experimental.pallas import tpu as pltpu

F32 = jnp.float32
BF16 = jnp.bfloat16
NEG_INF = float("-inf")

LANES = 128
HEAD_DIM = 128
MOBA_BLOCK = 256
MOBA_TOPK = 3
DSA_TOPK = 256
IDX_HEADS = 16
PEER_HEADS = 8
PEER_N_KEYS = 128
PEER_TOPK = 16
ROPE_THETA = 10000.0
RMS_EPS = 1e-6
VMEM_LIMIT = 56 * 1024 * 1024


def _cparams(sem):
    return pltpu.CompilerParams(dimension_semantics=sem, vmem_limit_bytes=VMEM_LIMIT)


def _dot_nt(a, b, precision=None):
    return lax.dot_general(a, b, (((1,), (1,)), ((), ())),
                           preferred_element_type=F32, precision=precision)


def _dot(a, b):
    return jnp.dot(a, b, preferred_element_type=F32)


def _rms(x, g):
    return x * lax.rsqrt(jnp.mean(x * x, axis=-1, keepdims=True) + RMS_EPS) * g


def _rmsnorm_kernel(x_ref, g_ref, o_ref):
    o_ref[...] = _rms(x_ref[...], g_ref[...]).astype(o_ref.dtype)


def rmsnorm_bf16(x, g, tm):
    M, D = x.shape
    return pl.pallas_call(
        _rmsnorm_kernel,
        grid=(M // tm,),
        in_specs=[pl.BlockSpec((tm, D), lambda i: (i, 0)),
                  pl.BlockSpec((1, D), lambda i: (0, 0))],
        out_specs=pl.BlockSpec((tm, D), lambda i: (i, 0)),
        out_shape=jax.ShapeDtypeStruct((M, D), BF16),
        compiler_params=_cparams(("parallel",)),
        name="rmsnorm",
    )(x, g.reshape(1, D))


def _cast_kernel(x_ref, o_ref, *, transpose):
    x = x_ref[...]
    o_ref[...] = (x.T if transpose else x).astype(o_ref.dtype)


def cast_bf16(x, tr, transpose=False):
    R, C = x.shape
    return pl.pallas_call(
        functools.partial(_cast_kernel, transpose=transpose),
        grid=(R // tr,),
        in_specs=[pl.BlockSpec((tr, C), lambda i: (i, 0))],
        out_specs=pl.BlockSpec((C, tr), lambda i: (0, i)) if transpose else pl.BlockSpec((tr, C), lambda i: (i, 0)),
        out_shape=jax.ShapeDtypeStruct((C, R) if transpose else (R, C), BF16),
        compiler_params=_cparams(("parallel",)),
        name="cast_t" if transpose else "cast",
    )(x)


def _proj_kernel(h_ref, w_ref, cos_ref, sin_ref, o_ref, *, rope, scale):
    acc = _dot(h_ref[...], w_ref[...])
    if scale != 1.0:
        acc = acc * scale
    if rope:
        cos = cos_ref[...]
        sin = sin_ref[...]
        for c in range(acc.shape[1] // HEAD_DIM):
            a = acc[:, c * HEAD_DIM:(c + 1) * HEAD_DIM]
            r = a * cos + pltpu.roll(a, HEAD_DIM // 2, 1) * sin
            o_ref[:, c * HEAD_DIM:(c + 1) * HEAD_DIM] = r.astype(o_ref.dtype)
    else:
        o_ref[...] = acc.astype(o_ref.dtype)


def project(h, w, cos, sin, *, rope, out_dtype, tm, tn, scale=1.0):
    M, K = h.shape
    N = w.shape[1]
    tn = min(tn, N)
    return pl.pallas_call(
        functools.partial(_proj_kernel, rope=rope, scale=scale),
        grid=(M // tm, N // tn),
        in_specs=[pl.BlockSpec((tm, K), lambda i, j: (i, 0)),
                  pl.BlockSpec((K, tn), lambda i, j: (0, j)),
                  pl.BlockSpec((tm, HEAD_DIM), lambda i, j: (i, 0)),
                  pl.BlockSpec((tm, HEAD_DIM), lambda i, j: (i, 0))],
        out_specs=pl.BlockSpec((tm, tn), lambda i, j: (i, j)),
        out_shape=jax.ShapeDtypeStruct((M, N), out_dtype),
        compiler_params=_cparams(("parallel", "arbitrary")),
        name="project_rope" if rope else "project",
    )(h, w, cos, sin)


def rope_tables(pos):
    half = HEAD_DIM // 2
    inv = ROPE_THETA ** (-jnp.arange(half, dtype=F32) / half)
    ang = pos.astype(F32)[:, None] * inv[None, :]
    cos = jnp.cos(ang)
    sin = jnp.sin(ang)
    return jnp.concatenate([cos, cos], axis=-1), jnp.concatenate([-sin, sin], axis=-1)


def _moba_prompt_kernel(q_ref, k_ref, v_ref, o_ref, kb_scr, vt_scr, mean_scr, sel_scr, *, nb):
    i = pl.program_id(2)
    blk = MOBA_BLOCK
    g = q_ref.shape[1] // HEAD_DIM
    nq = g * blk

    @pl.when(i == 0)
    def _():
        for j in range(nb):
            kj = k_ref[j * blk:(j + 1) * blk, :]
            mean_scr[j:j + 1, :] = jnp.sum(kj, axis=0, keepdims=True) / blk
            kb_scr[j] = kj.astype(BF16)
            vt_scr[j] = v_ref[j * blk:(j + 1) * blk, :].T.astype(BF16)

    q = q_ref[...]
    qs = jnp.concatenate([q[:, h * HEAD_DIM:(h + 1) * HEAD_DIM] for h in range(g)], axis=0)

    gate = _dot_nt(mean_scr[...], qs.astype(F32), precision=lax.Precision.HIGHEST)
    row = lax.broadcasted_iota(jnp.int32, (nb, nq), 0)
    valid = row < i
    gate = jnp.where(valid, gate, NEG_INF)
    rank = jnp.zeros((nb, nq), jnp.int32)
    for jp in range(nb):
        gj = gate[jp:jp + 1, :]
        beats = (gj > gate) | ((gj == gate) & (jp < row))
        rank = rank + beats.astype(jnp.int32)
    sel_scr[...] = jnp.where(valid & (rank < MOBA_TOPK), 1.0, 0.0).astype(F32)

    scale = HEAD_DIM ** -0.5
    s = _dot_nt(kb_scr[i], qs) * scale
    kpos = lax.broadcasted_iota(jnp.int32, (blk, nq), 0)
    qpos = lax.broadcasted_iota(jnp.int32, (blk, nq), 1) % blk
    s = jnp.where(kpos <= qpos, s, NEG_INF)
    m0 = jnp.max(s, axis=0, keepdims=True)
    p = jnp.exp(s - m0)
    l0 = jnp.sum(p, axis=0, keepdims=True)
    acc0 = _dot(vt_scr[i], p.astype(BF16))

    def body(j, carry):
        m, l, acc = carry
        sj = _dot_nt(kb_scr[j], qs) * scale
        selj = sel_scr[pl.ds(j, 1), :] > 0.5
        sj = jnp.where(selj, sj, NEG_INF)
        m_new = jnp.maximum(m, jnp.max(sj, axis=0, keepdims=True))
        pj = jnp.exp(sj - m_new)
        alpha = jnp.exp(m - m_new)
        l = alpha * l + jnp.sum(pj, axis=0, keepdims=True)
        acc = alpha * acc + _dot(vt_scr[j], pj.astype(BF16))
        return m_new, l, acc

    _, l, acc = lax.fori_loop(0, i, body, (m0, l0, acc0))
    out = acc / l
    for h in range(g):
        o_ref[:, h * HEAD_DIM:(h + 1) * HEAD_DIM] = out[:, h * blk:(h + 1) * blk].T.astype(o_ref.dtype)


def moba_prompt(q, k, v, B, T, H, q_col0=0):
    KVH = k.shape[1] // HEAD_DIM
    g = H // KVH
    nb = T // MOBA_BLOCK
    qb0 = q_col0 // (g * HEAD_DIM)
    return pl.pallas_call(
        functools.partial(_moba_prompt_kernel, nb=nb),
        grid=(B, KVH, nb),
        in_specs=[pl.BlockSpec((MOBA_BLOCK, g * HEAD_DIM), lambda b, k_, i: (b * nb + i, qb0 + k_)),
                  pl.BlockSpec((T, HEAD_DIM), lambda b, k_, i: (b, k_)),
                  pl.BlockSpec((T, HEAD_DIM), lambda b, k_, i: (b, k_))],
        out_specs=pl.BlockSpec((MOBA_BLOCK, g * HEAD_DIM), lambda b, k_, i: (b * nb + i, k_)),
        out_shape=jax.ShapeDtypeStruct((B * T, H * HEAD_DIM), BF16),
        scratch_shapes=[pltpu.VMEM((nb, MOBA_BLOCK, HEAD_DIM), BF16),
                        pltpu.VMEM((nb, HEAD_DIM, MOBA_BLOCK), BF16),
                        pltpu.VMEM((nb, HEAD_DIM), F32),
                        pltpu.VMEM((nb, g * MOBA_BLOCK), F32)],
        compiler_params=_cparams(("parallel", "parallel", "arbitrary")),
        name="moba_prompt",
    )(q, k, v)


def _ordered_key(x):
    u = pltpu.bitcast(x, jnp.int32)
    return u ^ ((u >> 31) & jnp.int32(0x7FFFFFFF))


def _dsa_prompt_kernel(qi_ref, wi_ref, q_ref, ki_ref, k_ref, v_ref, o_ref,
                       kib_scr, kb_scr, vt_scr, key_scr, sel_scr, *, nb, k_sel):
    i = pl.program_id(1)
    tq = qi_ref.shape[0]
    kvh_n = k_ref.shape[1] // HEAD_DIM
    h_n = q_ref.shape[1] // HEAD_DIM
    g = h_n // kvh_n
    hi_n = qi_ref.shape[1] // HEAD_DIM
    int_min = jnp.int32(-2 ** 31)

    @pl.when(i == 0)
    def _():
        for j in range(nb):
            rows = slice(j * tq, (j + 1) * tq)
            kib_scr[j] = ki_ref[rows, :].astype(BF16)
            for c in range(kvh_n):
                cols = slice(c * HEAD_DIM, (c + 1) * HEAD_DIM)
                kb_scr[c * nb + j] = k_ref[rows, cols].astype(BF16)
                vt_scr[c * nb + j] = v_ref[rows, cols].T.astype(BF16)

    wit = wi_ref[...].T * (hi_n * HEAD_DIM) ** -0.5
    qi = qi_ref[...]

    kpos = lax.broadcasted_iota(jnp.int32, (tq, tq), 0)
    qpos = lax.broadcasted_iota(jnp.int32, (tq, tq), 1)
    causal = kpos <= qpos

    def score_blk(j, _):
        kib = kib_scr[j]
        acc = jnp.zeros((tq, tq), F32)
        for h in range(hi_n):
            r = _dot_nt(kib, qi[:, h * HEAD_DIM:(h + 1) * HEAD_DIM])
            acc = acc + jnp.maximum(r, 0.0) * wit[h:h + 1, :]
        key = _ordered_key(acc)
        key = jnp.where((j < i) | causal, key, int_min)
        key_scr[j] = key
        return 0

    lax.fori_loop(0, i + 1, score_blk, 0)

    def count(pred_fn):
        def blk(j, c):
            m = pred_fn(key_scr[j], j)
            return c + jnp.sum(m.astype(jnp.int32).reshape(tq // 8, 8, tq), axis=0)
        c8 = lax.fori_loop(0, i + 1, blk, jnp.zeros((8, tq), jnp.int32))
        return jnp.sum(c8, axis=0, keepdims=True)

    def bit_step(it, t_u):
        bit = lax.shift_left(jnp.int32(1), 31 - it)
        cand_u = t_u | bit
        cand = cand_u ^ int_min
        c = count(lambda kk, j: kk >= cand)
        return jnp.where(c >= k_sel, cand_u, t_u)

    t_u = lax.fori_loop(0, 32, bit_step, jnp.zeros((1, tq), jnp.int32))
    thr = t_u ^ int_min
    need = k_sel - count(lambda kk, j: kk > thr)

    def idx_of(j):
        return lax.broadcasted_iota(jnp.int32, (tq, tq), 0) + j * tq

    n_bits = max(1, int(math.ceil(math.log2(nb * tq))))

    def tie_step(it, y):
        cand = y | lax.shift_left(jnp.int32(1), n_bits - 1 - it)
        c = count(lambda kk, j: (kk == thr) & (idx_of(j) < cand))
        return jnp.where(c < need, cand, y)

    surplus = jnp.max(count(lambda kk, j: kk == thr) - need)
    y = lax.cond(surplus > 0,
                 lambda: lax.fori_loop(0, n_bits, tie_step, jnp.zeros((1, tq), jnp.int32)),
                 lambda: jnp.full((1, tq), 2 ** n_bits - 1, jnp.int32))

    def sel_blk(j, _):
        kk = key_scr[j]
        s = (kk > thr) | ((kk == thr) & (idx_of(j) <= y))
        s = s & (kk > int_min)
        sel_scr[j] = jnp.where(s, 1.0, 0.0).astype(F32)
        return 0

    lax.fori_loop(0, i + 1, sel_blk, 0)

    scale = HEAD_DIM ** -0.5
    q = q_ref[...]
    for c in range(kvh_n):
        qs = jnp.concatenate([q[:, h * HEAD_DIM:(h + 1) * HEAD_DIM] for h in range(c * g, (c + 1) * g)], axis=0)

        def body(j, carry, c=c, qs=qs):
            m, l, acc = carry
            s = _dot_nt(kb_scr[c * nb + j], qs) * scale
            sel = jnp.concatenate([sel_scr[j]] * g, axis=1) > 0.5
            m_new = jnp.maximum(m, jnp.max(jnp.where(sel, s, -1e30), axis=0, keepdims=True))
            p = jnp.where(sel, jnp.exp(s - m_new), 0.0)
            alpha = jnp.exp(m - m_new)
            l = alpha * l + jnp.sum(p, axis=0, keepdims=True)
            acc = alpha * acc + _dot(vt_scr[c * nb + j], p.astype(BF16))
            return m_new, l, acc

        init = (jnp.full((1, g * tq), -1e30, F32), jnp.zeros((1, g * tq), F32),
                jnp.zeros((HEAD_DIM, g * tq), F32))
        _, l, acc = lax.fori_loop(0, i + 1, body, init)
        out = acc / l
        for hh in range(g):
            h = c * g + hh
            o_ref[:, h * HEAD_DIM:(h + 1) * HEAD_DIM] = out[:, hh * tq:(hh + 1) * tq].T.astype(o_ref.dtype)


def dsa_prompt(qi, wi, q, ki, k, v, B, T, H, qi_col0=0, q_col0=0):
    tq = 256
    nb = T // tq
    k_sel = min(DSA_TOPK, T // 4)
    kvh_n = k.shape[1] // HEAD_DIM
    wqi = IDX_HEADS * HEAD_DIM
    wq = H * HEAD_DIM
    qib, qb = qi_col0 // wqi, q_col0 // wq
    return pl.pallas_call(
        functools.partial(_dsa_prompt_kernel, nb=nb, k_sel=k_sel),
        grid=(B, nb),
        in_specs=[pl.BlockSpec((tq, wqi), lambda b, i: (b * nb + i, qib)),
                  pl.BlockSpec((tq, LANES), lambda b, i: (b * nb + i, 0)),
                  pl.BlockSpec((tq, wq), lambda b, i: (b * nb + i, qb)),
                  pl.BlockSpec((T, HEAD_DIM), lambda b, i: (b, 0)),
                  pl.BlockSpec((T, k.shape[1]), lambda b, i: (b, 0)),
                  pl.BlockSpec((T, v.shape[1]), lambda b, i: (b, 0))],
        out_specs=pl.BlockSpec((tq, wq), lambda b, i: (b * nb + i, 0)),
        out_shape=jax.ShapeDtypeStruct((B * T, wq), BF16),
        scratch_shapes=[pltpu.VMEM((nb, tq, HEAD_DIM), BF16),
                        pltpu.VMEM((kvh_n * nb, tq, HEAD_DIM), BF16),
                        pltpu.VMEM((kvh_n * nb, HEAD_DIM, tq), BF16),
                        pltpu.VMEM((nb, tq, tq), jnp.int32),
                        pltpu.VMEM((nb, tq, tq), F32)],
        compiler_params=_cparams(("parallel", "arbitrary")),
        name="dsa_prompt",
    )(qi, wi, q, ki, k, v)


def _resident(shape):
    nd = len(shape)
    return pl.BlockSpec(shape, lambda *_: (0,) * nd, pipeline_mode=pl.Buffered(1))


def _merge_kernel(oa_ref, ob_ref, ga_ref, gb_ref, x_ref, wa_ref, wb_ref, wo_ref, o_ref):
    ya = _dot(oa_ref[...], wa_ref[...])
    yb = _dot(ob_ref[...], wb_ref[...])
    z = jax.nn.sigmoid(ga_ref[...].astype(F32)) * ya + jax.nn.sigmoid(gb_ref[...].astype(F32)) * yb
    o_ref[...] = x_ref[...] + _dot(z.astype(BF16), wo_ref[...])


def merge(o_a, o_b, gates, x, w_a, w_b, w_o, tm):
    M, D = x.shape
    return pl.pallas_call(
        _merge_kernel,
        grid=(M // tm,),
        in_specs=[pl.BlockSpec((tm, o_a.shape[1]), lambda i: (i, 0)),
                  pl.BlockSpec((tm, o_b.shape[1]), lambda i: (i, 0)),
                  pl.BlockSpec((tm, D), lambda i: (i, 0)),
                  pl.BlockSpec((tm, D), lambda i: (i, 1)),
                  pl.BlockSpec((tm, D), lambda i: (i, 0)),
                  _resident(w_a.shape), _resident(w_b.shape), _resident(w_o.shape)],
        out_specs=pl.BlockSpec((tm, D), lambda i: (i, 0)),
        out_shape=jax.ShapeDtypeStruct((M, D), F32),
        compiler_params=_cparams(("parallel",)),
        name="merge",
    )(o_a, o_b, gates, gates, x, w_a, w_b, w_o)


RANK_NONE = 64


def _top_values(x, n, order=None, exact=True, want_rank=False):
    vals = []
    rank = jnp.full(x.shape, float(RANK_NONE), F32) if want_rank else None
    if exact:
        if order is None:
            order = lax.broadcasted_iota(jnp.int32, x.shape, 0)
        big = jnp.int32(2 ** 30)
        for a in range(n):
            m = jnp.max(x, axis=0, keepdims=True)
            vals.append(m)
            hit = order == jnp.min(jnp.where(x == m, order, big), axis=0, keepdims=True)
            x = jnp.where(hit, NEG_INF, x)
            if want_rank:
                rank = jnp.where(hit, float(a), rank)
        return jnp.concatenate(vals, axis=0), jnp.zeros((1, x.shape[1]), jnp.int32), rank
    y = x
    for a in range(n):
        m = jnp.max(y, axis=0, keepdims=True)
        vals.append(m)
        hit = y == m
        y = jnp.where(hit, NEG_INF, y)
        if want_rank:
            rank = jnp.where(hit, float(a), rank)
    removed = (jnp.sum(jnp.where(y == NEG_INF, 1, 0), axis=0, keepdims=True)
               - jnp.sum(jnp.where(x == NEG_INF, 1, 0), axis=0, keepdims=True))
    return jnp.concatenate(vals, axis=0), jnp.abs(removed - n), rank


def _pair_candidates(v1, v2):
    k = PEER_TOPK
    tm = v1.shape[1]
    sub = lax.broadcasted_iota(jnp.int32, (8, tm), 0)
    vals = [v1[0:1, :] + v2]
    order = [lax.broadcasted_iota(jnp.int32, (k, tm), 0)]
    for a in range(1, 8):
        vals.append(jnp.where(sub < k // (a + 1), v1[a:a + 1, :] + v2[0:8, :], NEG_INF))
        order.append(sub + a * k)
    vals.append(v1[8:16, :] + v2[0:1, :])
    order.append((sub + 8) * k)
    return jnp.concatenate(vals, axis=0), jnp.concatenate(order, axis=0)


def _peer_topk_kernel(x_ref, g_ref, wq_ref, k1_ref, k2_ref, ht_ref, n1_ref, e1_ref, r2_ref, e2_ref, q_scr):
    tm = x_ref.shape[0]
    hn = x_ref[...]
    hn = _rms(hn, g_ref[...])
    ht_ref[...] = hn.T.astype(BF16)
    q = _dot(hn.astype(BF16), wq_ref[...]).astype(BF16)
    kd = 2 * PEER_N_KEYS
    for hd in range(PEER_HEADS):
        q_scr[hd] = q[:, hd * kd:(hd + 1) * kd]

    def head(hd, _):
        qh = q_scr[hd]
        s1 = _dot_nt(k1_ref[hd], qh[:, :PEER_N_KEYS])
        s2 = _dot_nt(k2_ref[hd], qh[:, PEER_N_KEYS:])

        def gating(exact):
            outs, bad = [[], [], [], []], jnp.zeros((1, LANES), jnp.int32)
            for tc in range(tm // LANES):
                cols = slice(tc * LANES, (tc + 1) * LANES)
                s1c, s2c = s1[:, cols], s2[:, cols]
                v1, b1, rank1 = _top_values(s1c, PEER_TOPK, exact=exact, want_rank=True)
                v2, b2, rank2 = _top_values(s2c, PEER_TOPK, exact=exact, want_rank=True)
                cand, order = _pair_candidates(v1, v2)
                best, b3, _ = _top_values(cand, PEER_TOPK, order, exact=exact)
                thr = best[PEER_TOPK - 1:PEER_TOPK, :]
                inv_z = 1.0 / jnp.sum(jnp.exp(best - best[0:1, :]), axis=0, keepdims=True)
                n1 = jnp.zeros_like(s1c)
                for a in range(PEER_TOPK):
                    n_a = jnp.sum(jnp.where(v1[a:a + 1, :] + v2 >= thr, 1.0, 0.0), axis=0, keepdims=True)
                    n1 = jnp.where(rank1 == float(a), n_a, n1)
                outs[0].append(n1)
                outs[1].append(jnp.exp(s1c - v1[0:1, :]) * inv_z)
                outs[2].append(rank2)
                outs[3].append(jnp.exp(s2c - v2[0:1, :]))
                bad = bad + b1 + b2 + b3
            return tuple(jnp.concatenate(o, axis=1) for o in outs), bad

        fast, bad = gating(False)
        n1, e1, r2, e2 = lax.cond(jnp.max(bad) > 0, lambda: gating(True)[0], lambda: fast)
        n1_ref[hd] = n1
        e1_ref[hd] = e1
        r2_ref[hd] = r2.astype(BF16)
        e2_ref[hd] = e2.astype(BF16)
        return 0

    lax.fori_loop(0, PEER_HEADS, head, 0)


def peer_topk(x, g, w_q, keys1, keys2, tm):
    M, D = x.shape
    hp = PEER_HEADS
    per_key = pl.BlockSpec((hp, PEER_N_KEYS, tm), lambda i: (0, 0, i))
    return pl.pallas_call(
        _peer_topk_kernel,
        grid=(M // tm,),
        in_specs=[pl.BlockSpec((tm, D), lambda i: (i, 0)),
                  pl.BlockSpec((1, D), lambda i: (0, 0)),
                  _resident(w_q.shape), _resident(keys1.shape), _resident(keys2.shape)],
        out_specs=[pl.BlockSpec((D, tm), lambda i: (0, i)), per_key, per_key, per_key, per_key],
        out_shape=[jax.ShapeDtypeStruct((D, M), BF16),
                   jax.ShapeDtypeStruct((hp, PEER_N_KEYS, M), F32),
                   jax.ShapeDtypeStruct((hp, PEER_N_KEYS, M), F32),
                   jax.ShapeDtypeStruct((hp, PEER_N_KEYS, M), BF16),
                   jax.ShapeDtypeStruct((hp, PEER_N_KEYS, M), BF16)],
        scratch_shapes=[pltpu.VMEM((hp, tm, 2 * PEER_N_KEYS), BF16)],
        compiler_params=_cparams(("parallel",)),
        name="peer_topk",
    )(x, g.reshape(1, D), w_q, keys1, keys2)


def _peer_dense_kernel(ht_ref, n1_ref, e1_ref, r2_ref, e2_ref, u_ref, vt_ref, o_ref):
    e = pl.program_id(1)
    n_i1 = u_ref.shape[0] // PEER_N_KEYS

    @pl.when(e == 0)
    def _():
        o_ref[...] = jnp.zeros_like(o_ref)

    a = _dot(u_ref[...], ht_ref[...])
    act = (0.5 * a * (1.0 + lax.erf(a * (2.0 ** -0.5)))).astype(BF16)
    zero = jnp.zeros((), BF16)
    zs = []
    for r in range(n_i1):
        i1 = e * n_i1 + r
        w = None
        for hd in range(PEER_HEADS):
            n_row = n1_ref[hd, pl.ds(i1, 1), :].astype(BF16)
            e_row = e1_ref[hd, pl.ds(i1, 1), :].astype(BF16)
            c = jnp.where(r2_ref[hd] < n_row, e_row * e2_ref[hd], zero)
            w = c if w is None else w + c
        zs.append(act[r * PEER_N_KEYS:(r + 1) * PEER_N_KEYS, :] * w)
    z = jnp.concatenate(zs, axis=0)
    o_ref[...] += _dot(vt_ref[...], z)


def peer_dense(ht, n1, e1, r2, e2, u, vt, tm, e_blk):
    D, M = ht.shape
    E = u.shape[0]
    per_key = pl.BlockSpec((PEER_HEADS, PEER_N_KEYS, tm), lambda i, e: (0, 0, i))
    return pl.pallas_call(
        _peer_dense_kernel,
        grid=(M // tm, E // e_blk),
        in_specs=[pl.BlockSpec((D, tm), lambda i, e: (0, i)), per_key, per_key, per_key, per_key,
                  pl.BlockSpec((e_blk, D), lambda i, e: (e, 0)),
                  pl.BlockSpec((D, e_blk), lambda i, e: (0, e))],
        out_specs=pl.BlockSpec((D, tm), lambda i, e: (0, i)),
        out_shape=jax.ShapeDtypeStruct((D, M), F32),
        compiler_params=_cparams(("parallel", "arbitrary")),
        name="peer_dense",
    )(ht, n1, e1, r2, e2, u, vt)


def _ple_kernel(x_ref, pt_ref, p_ref, g_ref, wg_ref, wp_ref, gf_ref, o_ref, *, final_norm):
    x2 = x_ref[...] + pt_ref[...].T
    h = _rms(x2, g_ref[...]).astype(BF16)
    gate = jax.nn.sigmoid(_dot(h, wg_ref[...]))
    x3 = x2 + gate * _dot(p_ref[...].astype(BF16), wp_ref[...])
    if final_norm:
        x3 = _rms(x3, gf_ref[...])
    o_ref[...] = x3


def ple(x, peer_t, p, g, w_gate, w_proj, g_final, tm, final_norm):
    M, D = x.shape
    return pl.pallas_call(
        functools.partial(_ple_kernel, final_norm=final_norm),
        grid=(M // tm,),
        in_specs=[pl.BlockSpec((tm, D), lambda i: (i, 0)),
                  pl.BlockSpec((D, tm), lambda i: (0, i)),
                  pl.BlockSpec((tm, p.shape[1]), lambda i: (i, 0)),
                  pl.BlockSpec((1, D), lambda i: (0, 0)),
                  _resident(w_gate.shape), _resident(w_proj.shape),
                  pl.BlockSpec((1, D), lambda i: (0, 0))],
        out_specs=pl.BlockSpec((tm, D), lambda i: (i, 0)),
        out_shape=jax.ShapeDtypeStruct((M, D), F32),
        compiler_params=_cparams(("parallel",)),
        name="ple",
    )(x, peer_t, p, g.reshape(1, D), w_gate, w_proj, g_final.reshape(1, D))


SAMPLE_PPS = 8
SCAN_SLOTS = 4
NBP_PAD = 128


def _sample_scan_kernel(pt_ref, qi_ref, wi_ref, kin_ref, qa_ref, cak_ref, cki_ref, isc_ref, inew_ref, ids_ref,
                        ak_buf, ki_buf, sums_scr, sem, *, layer, pps, ps, nbp, n_chunks):
    b = pl.program_id(0)
    n_seq = pl.num_programs(0)
    kvh_n = sums_scr.shape[0]
    ppb = MOBA_BLOCK // ps
    qi = qi_ref[...]
    w = wi_ref[...]

    def copies(seq, ch, slot):
        out = []
        for r in range(pps):
            phys = pt_ref[seq, ch * pps + r]
            out.append(pltpu.make_async_copy(cak_ref.at[layer, phys], ak_buf.at[slot, r], sem.at[0, slot]))
            out.append(pltpu.make_async_copy(cki_ref.at[layer, phys], ki_buf.at[slot, r], sem.at[1, slot]))
        return out

    ahead = SCAN_SLOTS - 1

    @pl.when(b == 0)
    def _():
        for ch0 in range(ahead):
            for c in copies(0, ch0, ch0):
                c.start()

    sums_scr[...] = jnp.zeros_like(sums_scr)
    kin = kin_ref[...].astype(BF16).astype(F32)
    r_new = jnp.sum(qi.astype(F32) * kin, axis=1, keepdims=True)
    inew_ref[...] = jnp.broadcast_to(jnp.sum(jnp.maximum(r_new, 0.0) * w, axis=0, keepdims=True), inew_ref.shape)

    def chunk(ch, _):
        slot = lax.rem(ch, SCAN_SLOTS)
        nxt = ch + ahead
        nxt_slot = lax.rem(nxt, SCAN_SLOTS)

        @pl.when(nxt < n_chunks)
        def _():
            for c in copies(b, nxt, nxt_slot):
                c.start()

        @pl.when((nxt >= n_chunks) & (b + 1 < n_seq))
        def _():
            for c in copies(b + 1, nxt - n_chunks, nxt_slot):
                c.start()

        for c in copies(b, ch, slot):
            c.wait()
        for r in range(pps):
            x = ak_buf[slot, r]
            n = ps
            while n > 1:
                n //= 2
                x = x[:n] + x[n:]
            blk = lax.div(ch * pps + r, ppb)
            for c in range(kvh_n):
                sums_scr[c, pl.ds(blk, 1), :] += x[0, c:c + 1, :]
            sc = _dot_nt(qi, ki_buf[slot, r].astype(BF16))
            isc_ref[pl.ds(ch * pps + r, 1), :] = jnp.sum(jnp.maximum(sc, 0.0) * w, axis=0, keepdims=True)
        return 0

    lax.fori_loop(0, n_chunks, chunk, 0)

    qa = qa_ref[...].astype(F32)
    h_n = qa.shape[0]
    g = h_n // kvh_n
    row = lax.broadcasted_iota(jnp.int32, (h_n, NBP_PAD), 0)
    lane = lax.broadcasted_iota(jnp.int32, (h_n, NBP_PAD), 1)
    gate = jnp.full((h_n, NBP_PAD), NEG_INF, F32)
    for c in range(kvh_n):
        gc = _dot_nt(qa, sums_scr[c] / MOBA_BLOCK, precision=lax.Precision.HIGHEST)
        gate = jnp.where((row >= c * g) & (row < (c + 1) * g), gc, gate)
    gate = jnp.where(lane < nbp, gate, NEG_INF)
    ids = jnp.zeros((h_n, NBP_PAD), jnp.int32)
    for r in range(MOBA_TOPK):
        m = jnp.max(gate, axis=1, keepdims=True)
        first = jnp.min(jnp.where(gate == m, lane, NBP_PAD), axis=1, keepdims=True)
        ids = jnp.where(lane == r, first, ids)
        gate = jnp.where(lane == first, NEG_INF, gate)
    ids_ref[...] = ids


def sample_scan(page_table, cache_a_k, cache_b_kidx, layer, qi, wi, ki_new, qa):
    bs, n_pages = page_table.shape
    ps, kvh_n = cache_a_k.shape[2], cache_a_k.shape[3]
    pps = SAMPLE_PPS
    nbp = n_pages * ps // MOBA_BLOCK
    h_n, hi_n = qa.shape[1], qi.shape[1]
    per_seq = lambda shape: pl.BlockSpec((None,) + shape, lambda b, pt: (b,) + (0,) * len(shape))
    hbm = pl.BlockSpec(memory_space=pl.ANY)
    grid_spec = pltpu.PrefetchScalarGridSpec(
        num_scalar_prefetch=1,
        grid=(bs,),
        in_specs=[per_seq((hi_n, HEAD_DIM)), per_seq((hi_n, 1)), per_seq((1, HEAD_DIM)), per_seq((h_n, HEAD_DIM)),
                  hbm, hbm],
        out_specs=[per_seq((n_pages, ps)), per_seq((1, LANES)), per_seq((h_n, NBP_PAD))],
        scratch_shapes=[pltpu.VMEM((SCAN_SLOTS, pps, ps, kvh_n, HEAD_DIM), F32),
                        pltpu.VMEM((SCAN_SLOTS, pps, ps, HEAD_DIM), F32),
                        pltpu.VMEM((kvh_n, NBP_PAD, HEAD_DIM), F32),
                        pltpu.SemaphoreType.DMA((2, SCAN_SLOTS))],
    )
    return pl.pallas_call(
        functools.partial(_sample_scan_kernel, layer=layer, pps=pps, ps=ps, nbp=nbp, n_chunks=n_pages // pps),
        grid_spec=grid_spec,
        out_shape=[jax.ShapeDtypeStruct((bs, n_pages, ps), F32),
                   jax.ShapeDtypeStruct((bs, 1, LANES), F32),
                   jax.ShapeDtypeStruct((bs, h_n, NBP_PAD), jnp.int32)],
        compiler_params=_cparams(("arbitrary",)),
        name="sample_scan",
    )(page_table, qi, wi, ki_new, qa, cache_a_k, cache_b_kidx)


def _sample_select_kernel(isc_ref, inew_ref, pos_ref, sel_scr, new_scr, rank_scr, *, k_sel, past):
    bs, n_pages, ps = isc_ref.shape
    int_min = jnp.int32(-2 ** 31)
    keys = _ordered_key(isc_ref[...])
    knew = _ordered_key(inew_ref[...][:, :, 0:1])
    idx = (lax.broadcasted_iota(jnp.int32, keys.shape, 1) * ps
           + lax.broadcasted_iota(jnp.int32, keys.shape, 2))

    def count(m_cache, m_new):
        c = jnp.sum(m_cache.astype(jnp.int32), axis=1, keepdims=True)
        return jnp.sum(c, axis=2, keepdims=True) + m_new.astype(jnp.int32)

    def bit_step(it, t_u):
        cand_u = t_u | lax.shift_left(jnp.int32(1), 31 - it)
        cand = cand_u ^ int_min
        return jnp.where(count(keys >= cand, knew >= cand) >= k_sel, cand_u, t_u)

    thr = lax.fori_loop(0, 32, bit_step, jnp.zeros((bs, 1, 1), jnp.int32)) ^ int_min
    need = k_sel - count(keys > thr, knew > thr)
    n_bits = int(math.ceil(math.log2(past + 1)))

    def tie_step(it, y):
        cand = y | lax.shift_left(jnp.int32(1), n_bits - 1 - it)
        c = count((keys == thr) & (idx < cand), (knew == thr) & (past < cand))
        return jnp.where(c < need, cand, y)

    y = lax.fori_loop(0, n_bits, tie_step, jnp.zeros((bs, 1, 1), jnp.int32))
    sel_scr[...] = jnp.where((keys > thr) | ((keys == thr) & (idx <= y)), 1.0, 0.0).astype(F32)
    new_sel = (knew > thr) | ((knew == thr) & (past <= y))
    new_scr[...] = jnp.broadcast_to(jnp.where(new_sel, 1.0, 0.0).astype(F32), new_scr.shape)

    upper = (lax.broadcasted_iota(jnp.int32, (ps, ps), 0) < lax.broadcasted_iota(jnp.int32, (ps, ps), 1))
    lower = (lax.broadcasted_iota(jnp.int32, (n_pages, n_pages), 1)
             < lax.broadcasted_iota(jnp.int32, (n_pages, n_pages), 0))
    upper = jnp.where(upper, 1.0, 0.0).astype(BF16)
    lower = jnp.where(lower, 1.0, 0.0).astype(BF16)
    r_iota = lax.broadcasted_iota(jnp.int32, (k_sel, ps), 0).astype(F32)
    slot = lax.broadcasted_iota(jnp.int32, (k_sel, ps), 1).astype(F32)
    lane = lax.broadcasted_iota(jnp.int32, (1, k_sel), 1)

    def per_seq(b, _):
        sb = sel_scr[b]
        prefix = _dot(sb.astype(BF16), upper)
        cnt = jnp.sum(sb, axis=1, keepdims=True)
        offs = _dot(lower, jnp.broadcast_to(cnt, sb.shape).astype(BF16))
        rank_scr[...] = offs + prefix

        def per_page(p, acc):
            hit = (r_iota == rank_scr[pl.ds(p, 1), :]) & (sel_scr[b, pl.ds(p, 1), :] > 0.5)
            return acc + jnp.where(hit, lax.convert_element_type(p * ps, F32) + slot, 0.0)

        acc = lax.fori_loop(0, n_pages, per_page, jnp.zeros((k_sel, ps), F32))
        tot = jnp.sum(acc.T, axis=0, keepdims=True)
        tot = tot + jnp.where(lane == k_sel - 1, new_scr[b][:, 0:1] * past, 0.0)
        pos_ref[b] = tot.astype(jnp.int32)
        return 0

    lax.fori_loop(0, bs, per_seq, 0)


def sample_select(iscore, inew, k_sel, past):
    bs, n_pages, ps = iscore.shape
    return pl.pallas_call(
        functools.partial(_sample_select_kernel, k_sel=k_sel, past=past),
        out_shape=jax.ShapeDtypeStruct((bs, 1, k_sel), jnp.int32),
        scratch_shapes=[pltpu.VMEM((bs, n_pages, ps), F32),
                        pltpu.VMEM((bs, 1, LANES), F32),
                        pltpu.VMEM((n_pages, ps), F32)],
        compiler_params=pltpu.CompilerParams(vmem_limit_bytes=VMEM_LIMIT),
        name="sample_select",
    )(iscore, inew)


def _sample_attend_kernel(pt_ref, ids_ref, pos_ref, qa_ref, kan_ref, van_ref, qb_ref, kbn_ref, vbn_ref,
                          cak_ref, cav_ref, cbk_ref, cbv_ref, oa_ref, ob_ref,
                          ka_buf, va_buf, kb_buf, vb_buf, sem, *, layer, ps, past, k_sel):
    b = pl.program_id(0)
    n_seq = pl.num_programs(0)
    h_n = qa_ref.shape[0]
    kvh_n = kan_ref.shape[0]
    g = h_n // kvh_n
    ppb = MOBA_BLOCK // ps
    scale = HEAD_DIM ** -0.5
    slot = lax.rem(b, 2)

    def moba_copies(seq, sl, h, r, half):
        phys = pt_ref[seq, ids_ref[seq, h * MOBA_TOPK + r] * ppb + half]
        rows = pl.ds((r * ppb + half) * ps, ps)
        return (pltpu.make_async_copy(cak_ref.at[layer, phys, :, h // g, :], ka_buf.at[sl, h, rows, :], sem.at[0, sl]),
                pltpu.make_async_copy(cav_ref.at[layer, phys, :, h // g, :], va_buf.at[sl, h, rows, :], sem.at[1, sl]))

    def dsa_copies(seq, sl, j):
        s = jnp.minimum(pos_ref[seq, j], past - 1)
        phys = pt_ref[seq, lax.div(s, ps)]
        row = lax.rem(s, ps)
        return (pltpu.make_async_copy(cbk_ref.at[layer, phys, row], kb_buf.at[sl, :, j, :], sem.at[2, sl]),
                pltpu.make_async_copy(cbv_ref.at[layer, phys, row], vb_buf.at[sl, :, j, :], sem.at[3, sl]))

    moba_keys = [(h, r, half) for h in range(h_n) for r in range(MOBA_TOPK) for half in range(ppb)]

    def transfer(seq, sl, start):
        for key in moba_keys:
            for c in moba_copies(seq, sl, *key):
                c.start() if start else c.wait()

        def row(j, _):
            for c in dsa_copies(seq, sl, j):
                c.start() if start else c.wait()
            return 0

        lax.fori_loop(0, k_sel, row, 0)

    @pl.when(b == 0)
    def _():
        transfer(0, 0, True)

    @pl.when(b + 1 < n_seq)
    def _():
        transfer(b + 1, 1 - slot, True)

    transfer(b, slot, False)
    ka_buf, va_buf, kb_buf, vb_buf = ka_buf.at[slot], va_buf.at[slot], kb_buf.at[slot], vb_buf.at[slot]

    @pl.when(pos_ref[b, k_sel - 1] == past)
    def _():
        for c in range(kvh_n):
            kb_buf[c, k_sel - 1:k_sel, :] = kbn_ref[c:c + 1, :]
            vb_buf[c, k_sel - 1:k_sel, :] = vbn_ref[c:c + 1, :]

    qa = qa_ref[...]
    qaf = qa.astype(F32)
    kan = kan_ref[...].astype(BF16).astype(F32)
    van = van_ref[...].astype(BF16).astype(F32)
    for h in range(h_n):
        c = h // g
        s = _dot_nt(qa, ka_buf[h].astype(BF16))[h:h + 1, :] * scale
        s_new = jnp.sum(qaf[h:h + 1, :] * kan[c:c + 1, :], axis=1, keepdims=True) * scale
        m = jnp.maximum(jnp.max(s, axis=1, keepdims=True), s_new)
        p = jnp.exp(s - m)
        p_new = jnp.exp(s_new - m)
        l = jnp.sum(p, axis=1, keepdims=True) + p_new
        pv = _dot(jnp.broadcast_to(p, (8, p.shape[1])).astype(BF16), va_buf[h].astype(BF16))[0:1, :]
        oa_ref[h:h + 1, :] = (pv + p_new * van[c:c + 1, :]) / l

    qb = qb_ref[...]
    for c in range(kvh_n):
        s = _dot_nt(qb, kb_buf[c].astype(BF16)) * scale
        m = jnp.max(s, axis=1, keepdims=True)
        p = jnp.exp(s - m)
        l = jnp.sum(p, axis=1, keepdims=True)
        o = _dot(p.astype(BF16), vb_buf[c].astype(BF16)) / l
        ob_ref[c * g:(c + 1) * g, :] = o[c * g:(c + 1) * g, :]


def sample_attend(page_table, ids, pos, qa, ka_new, va_new, qb, kb_new, vb_new,
                  cache_a_k, cache_a_v, cache_b_k, cache_b_v, layer):
    bs, n_pages = page_table.shape
    ps, kvh_n = cache_a_k.shape[2], cache_a_k.shape[3]
    past = n_pages * ps
    k_sel = pos.shape[1]
    h_n = qa.shape[1]
    per_seq = lambda shape: pl.BlockSpec((None,) + shape, lambda b, *_: (b,) + (0,) * len(shape))
    hbm = pl.BlockSpec(memory_space=pl.ANY)
    grid_spec = pltpu.PrefetchScalarGridSpec(
        num_scalar_prefetch=3,
        grid=(bs,),
        in_specs=[per_seq((h_n, HEAD_DIM)), per_seq((kvh_n, HEAD_DIM)), per_seq((kvh_n, HEAD_DIM)),
                  per_seq((h_n, HEAD_DIM)), per_seq((kvh_n, HEAD_DIM)), per_seq((kvh_n, HEAD_DIM)),
                  hbm, hbm, hbm, hbm],
        out_specs=[per_seq((h_n, HEAD_DIM)), per_seq((h_n, HEAD_DIM))],
        scratch_shapes=[pltpu.VMEM((2, h_n, MOBA_TOPK * MOBA_BLOCK, HEAD_DIM), F32),
                        pltpu.VMEM((2, h_n, MOBA_TOPK * MOBA_BLOCK, HEAD_DIM), F32),
                        pltpu.VMEM((2, kvh_n, k_sel, HEAD_DIM), F32),
                        pltpu.VMEM((2, kvh_n, k_sel, HEAD_DIM), F32),
                        pltpu.SemaphoreType.DMA((4, 2))],
    )
    return pl.pallas_call(
        functools.partial(_sample_attend_kernel, layer=layer, ps=ps, past=past, k_sel=k_sel),
        grid_spec=grid_spec,
        out_shape=[jax.ShapeDtypeStruct((bs, h_n, HEAD_DIM), F32)] * 2,
        compiler_params=_cparams(("arbitrary",)),
        name="sample_attend",
    )(page_table, ids, pos, qa, ka_new, va_new, qb, kb_new, vb_new, cache_a_k, cache_a_v, cache_b_k, cache_b_v)


def _sample_mixers(qa, ka, va, qb, kb, vb, qi, ki, wi, cache_a_k, cache_a_v, cache_b_k, cache_b_v,
                   cache_b_kidx, layer, page_table):
    bs = qa.shape[0]
    ps = cache_a_k.shape[2]
    n_pages = page_table.shape[1]
    past = n_pages * ps
    assert MOBA_BLOCK % ps == 0 and past % MOBA_BLOCK == 0 and n_pages % (SCAN_SLOTS * SAMPLE_PPS) == 0
    assert MOBA_TOPK <= past // MOBA_BLOCK <= NBP_PAD and SAMPLE_PPS % (MOBA_BLOCK // ps) == 0
    k_sel = min(DSA_TOPK, (past + 1) // 4)
    assert k_sel == DSA_TOPK
    r3 = lambda a: a.reshape(bs, -1, HEAD_DIM)
    wi3 = (wi[:, :IDX_HEADS] * (IDX_HEADS * HEAD_DIM) ** -0.5).reshape(bs, IDX_HEADS, 1)
    iscore, inew, ids = sample_scan(page_table, cache_a_k, cache_b_kidx, layer, r3(qi), wi3,
                                    ki.reshape(bs, 1, HEAD_DIM), r3(qa))
    pos = sample_select(iscore, inew, k_sel, past).reshape(bs, k_sel)
    ids = ids[:, :, :MOBA_TOPK].reshape(bs, -1)
    o_a, o_b = sample_attend(page_table, ids, pos, r3(qa), r3(ka), r3(va), r3(qb), r3(kb), r3(vb),
                             cache_a_k, cache_a_v, cache_b_k, cache_b_v, layer)
    return o_a.reshape(bs, -1), o_b.reshape(bs, -1)


N_HEADS_A = 8
N_KV_A = 4
N_HEADS_B = 8
N_KV_B = 4
PLE_PAD_ROWS = 128


def _split_w_in(w):
    d = HEAD_DIM
    sizes = (N_HEADS_A * d, N_KV_A * d, N_KV_A * d, N_HEADS_B * d, N_KV_B * d, N_KV_B * d,
             IDX_HEADS * d, d, IDX_HEADS, w.shape[0], w.shape[0])
    offs = np.concatenate([[0], np.cumsum(sizes)])
    qa, ka, va, qb, kb, vb, qi, ki, wi, ga, gb = (w[:, offs[n]:offs[n + 1]] for n in range(len(sizes)))
    wi = jnp.pad(wi, ((0, 0), (0, LANES - IDX_HEADS)))
    c = lambda *a: jnp.concatenate(a, axis=1).astype(BF16)
    return dict(qq=c(qi, qa, qb), gates=c(ga, gb), ka=c(ka), va=c(va), kb=c(kb), vb=c(vb), ki=c(ki), wi=c(wi))


def _project_all(h, w, cos, sin, tm):
    kw = dict(tm=tm, tn=512)
    return dict(
        qq=project(h, w['qq'], cos, sin, rope=True, out_dtype=BF16, **kw),
        gates=project(h, w['gates'], cos, sin, rope=False, out_dtype=BF16, **kw),
        ka=project(h, w['ka'], cos, sin, rope=True, out_dtype=F32, **kw),
        va=project(h, w['va'], cos, sin, rope=False, out_dtype=F32, **kw),
        kb=project(h, w['kb'], cos, sin, rope=True, out_dtype=F32, **kw),
        vb=project(h, w['vb'], cos, sin, rope=False, out_dtype=F32, **kw),
        ki=project(h, w['ki'], cos, sin, rope=True, out_dtype=F32, **kw),
        wi=project(h, w['wi'], cos, sin, rope=False, out_dtype=F32, **kw),
    )


def _post_mixer(x1, p, lw, g_final, final_norm, tm_topk, tm_dense, tm_ple):
    ht, n1, e1, r2, e2 = peer_topk(x1, lw['g_ffn'], lw['w_peer_q'], lw['keys1'], lw['keys2'], tm_topk)
    pt = peer_dense(ht, n1, e1, r2, e2, lw['u'], lw['vt'], tm_dense, 1024)
    return ple(x1, pt, p, lw['g_ple'], lw['w_ple_gate'], lw['w_ple_proj'], g_final, tm_ple, final_norm)


def kernel(x_prompt, x_sample, cache_a_k, cache_a_v, cache_b_k, cache_b_v, cache_b_kidx, page_table,
           p_prompt, p_sample, g_attn, w_in, w_branch_a, w_branch_b, w_out, g_ffn, w_peer_q,
           peer_keys1, peer_keys2, peer_u, peer_v, g_ple, w_ple_gate, w_ple_proj, g_final):
    B, T, D = x_prompt.shape
    bs, ts, _ = x_sample.shape
    assert ts == 1
    depth = w_in.shape[0]
    past = page_table.shape[1] * cache_a_k.shape[2]
    mp = B * T
    ms = PLE_PAD_ROWS
    qi_w = IDX_HEADS * HEAD_DIM
    qa_w = N_HEADS_A * HEAD_DIM

    cos_p, sin_p = rope_tables(jnp.arange(T, dtype=jnp.int32))
    cos_p, sin_p = jnp.tile(cos_p, (B, 1)), jnp.tile(sin_p, (B, 1))
    cos_s, sin_s = rope_tables(jnp.full((ms,), past, jnp.int32))

    xp = x_prompt.reshape(mp, D)
    xs = jnp.pad(x_sample.reshape(bs, D), ((0, ms - bs), (0, 0)))
    rows_p = [[] for _ in range(5)]
    rows_s = [[] for _ in range(5)]
    for i in range(depth):
        w = _split_w_in(w_in[i])
        lw = dict(g_ffn=g_ffn[i], w_peer_q=w_peer_q[i].astype(BF16), keys1=peer_keys1[i].astype(BF16),
                  keys2=peer_keys2[i].astype(BF16), u=cast_bf16(peer_u[i], 1024), vt=cast_bf16(peer_v[i], 1024, transpose=True),
                  g_ple=g_ple[i], w_ple_gate=w_ple_gate[i].astype(BF16), w_ple_proj=w_ple_proj[i].astype(BF16))
        w_a, w_b, w_o = w_branch_a[i].astype(BF16), w_branch_b[i].astype(BF16), w_out[i].astype(BF16)
        last = i == depth - 1

        sr = _project_all(rmsnorm_bf16(xs, g_attn[i], ms), w, cos_s, sin_s, ms)
        f = lambda a: a[:bs]
        qq = f(sr['qq'])
        so_a, so_b = _sample_mixers(
            qq[:, qi_w:qi_w + qa_w], f(sr['ka']), f(sr['va']), qq[:, qi_w + qa_w:], f(sr['kb']), f(sr['vb']),
            qq[:, :qi_w], f(sr['ki']), f(sr['wi']),
            cache_a_k, cache_a_v, cache_b_k, cache_b_v, cache_b_kidx, i, page_table)
        padr = lambda a: jnp.pad(a, ((0, ms - bs), (0, 0))).astype(BF16)
        x1s = merge(padr(so_a), padr(so_b), sr['gates'], xs, w_a, w_b, w_o, ms)
        ps = jnp.pad(p_sample[i].reshape(bs, -1), ((0, ms - bs), (0, 0)))
        xs = _post_mixer(x1s, ps, lw, g_final, last, ms, ms, ms)
        for lst, name in zip(rows_s, ('ka', 'va', 'kb', 'vb', 'ki')):
            r = sr[name][:bs]
            lst.append(r.reshape((bs, 1, -1, HEAD_DIM) if name != 'ki' else (bs, 1, HEAD_DIM)))

        pr = _project_all(rmsnorm_bf16(xp, g_attn[i], 512), w, cos_p, sin_p, 1024)
        o_a = moba_prompt(pr['qq'], pr['ka'], pr['va'], B, T, N_HEADS_A, q_col0=qi_w)
        o_b = dsa_prompt(pr['qq'], pr['wi'], pr['qq'], pr['ki'], pr['kb'], pr['vb'], B, T, N_HEADS_B,
                         qi_col0=0, q_col0=qi_w + qa_w)
        x1 = merge(o_a, o_b, pr['gates'], xp, w_a, w_b, w_o, 256)
        xp = _post_mixer(x1, p_prompt[i].reshape(mp, -1), lw, g_final, last, 256, 512, 256)
        for lst, name in zip(rows_p, ('ka', 'va', 'kb', 'vb', 'ki')):
            lst.append(pr[name].reshape((B, T, -1, HEAD_DIM) if name != 'ki' else (B, T, HEAD_DIM)))

    y_prompt = xp.reshape(B, T, D)
    y_sample = xs[:bs].reshape(bs, 1, D)
    return (y_prompt, y_sample) + tuple(jnp.stack(r) for r in rows_p) + tuple(jnp.stack(r) for r in rows_s)
```

```python
import functools
import math

import jax
import jax.numpy as jnp
import numpy as np
from jax import lax
from jax.experimental import pallas as pl
from jax.experimental.pallas import tpu as pltpu

F32 = jnp.float32
BF16 = jnp.bfloat16
NEG_INF = float("-inf")

LANES = 128
HEAD_DIM = 128
MOBA_BLOCK = 256
MOBA_TOPK = 3
DSA_TOPK = 256
IDX_HEADS = 16
PEER_HEADS = 8
PEER_N_KEYS = 128
PEER_TOPK = 16
ROPE_THETA = 10000.0
RMS_EPS = 1e-6
VMEM_LIMIT = 56 * 1024 * 1024


def _cparams(sem):
    return pltpu.CompilerParams(dimension_semantics=sem, vmem_limit_bytes=VMEM_LIMIT)


def _dot_nt(a, b, precision=None):
    return lax.dot_general(a, b, (((1,), (1,)), ((), ())),
                           preferred_element_type=F32, precision=precision)


def _dot(a, b):
    return jnp.dot(a, b, preferred_element_type=F32)


def _rms(x, g):
    return x * lax.rsqrt(jnp.mean(x * x, axis=-1, keepdims=True) + RMS_EPS) * g


def _rmsnorm_kernel(x_ref, g_ref, o_ref):
    o_ref[...] = _rms(x_ref[...], g_ref[...]).astype(o_ref.dtype)


def rmsnorm_bf16(x, g, tm):
    M, D = x.shape
    return pl.pallas_call(
        _rmsnorm_kernel,
        grid=(M // tm,),
        in_specs=[pl.BlockSpec((tm, D), lambda i: (i, 0)),
                  pl.BlockSpec((1, D), lambda i: (0, 0))],
        out_specs=pl.BlockSpec((tm, D), lambda i: (i, 0)),
        out_shape=jax.ShapeDtypeStruct((M, D), BF16),
        compiler_params=_cparams(("parallel",)),
        name="rmsnorm",
    )(x, g.reshape(1, D))


def _cast_kernel(x_ref, o_ref, *, transpose):
    x = x_ref[...]
    o_ref[...] = (x.T if transpose else x).astype(o_ref.dtype)


def cast_bf16(x, tr, transpose=False):
    R, C = x.shape
    return pl.pallas_call(
        functools.partial(_cast_kernel, transpose=transpose),
        grid=(R // tr,),
        in_specs=[pl.BlockSpec((tr, C), lambda i: (i, 0))],
        out_specs=pl.BlockSpec((C, tr), lambda i: (0, i)) if transpose else pl.BlockSpec((tr, C), lambda i: (i, 0)),
        out_shape=jax.ShapeDtypeStruct((C, R) if transpose else (R, C), BF16),
        compiler_params=_cparams(("parallel",)),
        name="cast_t" if transpose else "cast",
    )(x)


def _proj_kernel(h_ref, w_ref, cos_ref, sin_ref, o_ref, *, rope, scale):
    acc = _dot(h_ref[...], w_ref[...])
    if scale != 1.0:
        acc = acc * scale
    if rope:
        cos = cos_ref[...]
        sin = sin_ref[...]
        for c in range(acc.shape[1] // HEAD_DIM):
            a = acc[:, c * HEAD_DIM:(c + 1) * HEAD_DIM]
            r = a * cos + pltpu.roll(a, HEAD_DIM // 2, 1) * sin
            o_ref[:, c * HEAD_DIM:(c + 1) * HEAD_DIM] = r.astype(o_ref.dtype)
    else:
        o_ref[...] = acc.astype(o_ref.dtype)


def project(h, w, cos, sin, *, rope, out_dtype, tm, tn, scale=1.0):
    M, K = h.shape
    N = w.shape[1]
    tn = min(tn, N)
    return pl.pallas_call(
        functools.partial(_proj_kernel, rope=rope, scale=scale),
        grid=(M // tm, N // tn),
        in_specs=[pl.BlockSpec((tm, K), lambda i, j: (i, 0)),
                  pl.BlockSpec((K, tn), lambda i, j: (0, j)),
                  pl.BlockSpec((tm, HEAD_DIM), lambda i, j: (i, 0)),
                  pl.BlockSpec((tm, HEAD_DIM), lambda i, j: (i, 0))],
        out_specs=pl.BlockSpec((tm, tn), lambda i, j: (i, j)),
        out_shape=jax.ShapeDtypeStruct((M, N), out_dtype),
        compiler_params=_cparams(("parallel", "arbitrary")),
        name="project_rope" if rope else "project",
    )(h, w, cos, sin)


def rope_tables(pos):
    half = HEAD_DIM // 2
    inv = ROPE_THETA ** (-jnp.arange(half, dtype=F32) / half)
    ang = pos.astype(F32)[:, None] * inv[None, :]
    cos = jnp.cos(ang)
    sin = jnp.sin(ang)
    return jnp.concatenate([cos, cos], axis=-1), jnp.concatenate([-sin, sin], axis=-1)


def _moba_prompt_kernel(q_ref, k_ref, v_ref, o_ref, kb_scr, vt_scr, mean_scr, sel_scr, *, nb):
    i = pl.program_id(2)
    blk = MOBA_BLOCK
    g = q_ref.shape[1] // HEAD_DIM
    nq = g * blk

    @pl.when(i == 0)
    def _():
        for j in range(nb):
            kj = k_ref[j * blk:(j + 1) * blk, :]
            mean_scr[j:j + 1, :] = jnp.sum(kj, axis=0, keepdims=True) / blk
            kb_scr[j] = kj.astype(BF16)
            vt_scr[j] = v_ref[j * blk:(j + 1) * blk, :].T.astype(BF16)

    q = q_ref[...]
    qs = jnp.concatenate([q[:, h * HEAD_DIM:(h + 1) * HEAD_DIM] for h in range(g)], axis=0)

    gate = _dot_nt(mean_scr[...], qs.astype(F32), precision=lax.Precision.HIGHEST)
    row = lax.broadcasted_iota(jnp.int32, (nb, nq), 0)
    valid = row < i
    gate = jnp.where(valid, gate, NEG_INF)
    rank = jnp.zeros((nb, nq), jnp.int32)
    for jp in range(nb):
        gj = gate[jp:jp + 1, :]
        beats = (gj > gate) | ((gj == gate) & (jp < row))
        rank = rank + beats.astype(jnp.int32)
    sel_scr[...] = jnp.where(valid & (rank < MOBA_TOPK), 1.0, 0.0).astype(F32)

    scale = HEAD_DIM ** -0.5
    s = _dot_nt(kb_scr[i], qs) * scale
    kpos = lax.broadcasted_iota(jnp.int32, (blk, nq), 0)
    qpos = lax.broadcasted_iota(jnp.int32, (blk, nq), 1) % blk
    s = jnp.where(kpos <= qpos, s, NEG_INF)
    m0 = jnp.max(s, axis=0, keepdims=True)
    p = jnp.exp(s - m0)
    l0 = jnp.sum(p, axis=0, keepdims=True)
    acc0 = _dot(vt_scr[i], p.astype(BF16))

    def body(j, carry):
        m, l, acc = carry
        sj = _dot_nt(kb_scr[j], qs) * scale
        selj = sel_scr[pl.ds(j, 1), :] > 0.5
        sj = jnp.where(selj, sj, NEG_INF)
        m_new = jnp.maximum(m, jnp.max(sj, axis=0, keepdims=True))
        pj = jnp.exp(sj - m_new)
        alpha = jnp.exp(m - m_new)
        l = alpha * l + jnp.sum(pj, axis=0, keepdims=True)
        acc = alpha * acc + _dot(vt_scr[j], pj.astype(BF16))
        return m_new, l, acc

    _, l, acc = lax.fori_loop(0, i, body, (m0, l0, acc0))
    out = acc / l
    for h in range(g):
        o_ref[:, h * HEAD_DIM:(h + 1) * HEAD_DIM] = out[:, h * blk:(h + 1) * blk].T.astype(o_ref.dtype)


def moba_prompt(q, k, v, B, T, H, q_col0=0):
    KVH = k.shape[1] // HEAD_DIM
    g = H // KVH
    nb = T // MOBA_BLOCK
    qb0 = q_col0 // (g * HEAD_DIM)
    return pl.pallas_call(
        functools.partial(_moba_prompt_kernel, nb=nb),
        grid=(B, KVH, nb),
        in_specs=[pl.BlockSpec((MOBA_BLOCK, g * HEAD_DIM), lambda b, k_, i: (b * nb + i, qb0 + k_)),
                  pl.BlockSpec((T, HEAD_DIM), lambda b, k_, i: (b, k_)),
                  pl.BlockSpec((T, HEAD_DIM), lambda b, k_, i: (b, k_))],
        out_specs=pl.BlockSpec((MOBA_BLOCK, g * HEAD_DIM), lambda b, k_, i: (b * nb + i, k_)),
        out_shape=jax.ShapeDtypeStruct((B * T, H * HEAD_DIM), BF16),
        scratch_shapes=[pltpu.VMEM((nb, MOBA_BLOCK, HEAD_DIM), BF16),
                        pltpu.VMEM((nb, HEAD_DIM, MOBA_BLOCK), BF16),
                        pltpu.VMEM((nb, HEAD_DIM), F32),
                        pltpu.VMEM((nb, g * MOBA_BLOCK), F32)],
        compiler_params=_cparams(("parallel", "parallel", "arbitrary")),
        name="moba_prompt",
    )(q, k, v)


def _ordered_key(x):
    u = pltpu.bitcast(x, jnp.int32)
    return u ^ ((u >> 31) & jnp.int32(0x7FFFFFFF))


def _dsa_prompt_kernel(qi_ref, wi_ref, q_ref, ki_ref, k_ref, v_ref, o_ref,
                       kib_scr, kb_scr, vt_scr, key_scr, sel_scr, *, nb, k_sel):
    i = pl.program_id(1)
    tq = qi_ref.shape[0]
    kvh_n = k_ref.shape[1] // HEAD_DIM
    h_n = q_ref.shape[1] // HEAD_DIM
    g = h_n // kvh_n
    hi_n = qi_ref.shape[1] // HEAD_DIM
    int_min = jnp.int32(-2 ** 31)

    @pl.when(i == 0)
    def _():
        for j in range(nb):
            rows = slice(j * tq, (j + 1) * tq)
            kib_scr[j] = ki_ref[rows, :].astype(BF16)
            for c in range(kvh_n):
                cols = slice(c * HEAD_DIM, (c + 1) * HEAD_DIM)
                kb_scr[c * nb + j] = k_ref[rows, cols].astype(BF16)
                vt_scr[c * nb + j] = v_ref[rows, cols].T.astype(BF16)

    wit = wi_ref[...].T * (hi_n * HEAD_DIM) ** -0.5
    qi = qi_ref[...]

    kpos = lax.broadcasted_iota(jnp.int32, (tq, tq), 0)
    qpos = lax.broadcasted_iota(jnp.int32, (tq, tq), 1)
    causal = kpos <= qpos

    def score_blk(j, _):
        kib = kib_scr[j]
        acc = jnp.zeros((tq, tq), F32)
        for h in range(hi_n):
            r = _dot_nt(kib, qi[:, h * HEAD_DIM:(h + 1) * HEAD_DIM])
            acc = acc + jnp.maximum(r, 0.0) * wit[h:h + 1, :]
        key = _ordered_key(acc)
        key = jnp.where((j < i) | causal, key, int_min)
        key_scr[j] = key
        return 0

    lax.fori_loop(0, i + 1, score_blk, 0)

    def count(pred_fn):
        def blk(j, c):
            m = pred_fn(key_scr[j], j)
            return c + jnp.sum(m.astype(jnp.int32).reshape(tq // 8, 8, tq), axis=0)
        c8 = lax.fori_loop(0, i + 1, blk, jnp.zeros((8, tq), jnp.int32))
        return jnp.sum(c8, axis=0, keepdims=True)

    def bit_step(it, t_u):
        bit = lax.shift_left(jnp.int32(1), 31 - it)
        cand_u = t_u | bit
        cand = cand_u ^ int_min
        c = count(lambda kk, j: kk >= cand)
        return jnp.where(c >= k_sel, cand_u, t_u)

    t_u = lax.fori_loop(0, 32, bit_step, jnp.zeros((1, tq), jnp.int32))
    thr = t_u ^ int_min
    need = k_sel - count(lambda kk, j: kk > thr)

    def idx_of(j):
        return lax.broadcasted_iota(jnp.int32, (tq, tq), 0) + j * tq

    n_bits = max(1, int(math.ceil(math.log2(nb * tq))))

    def tie_step(it, y):
        cand = y | lax.shift_left(jnp.int32(1), n_bits - 1 - it)
        c = count(lambda kk, j: (kk == thr) & (idx_of(j) < cand))
        return jnp.where(c < need, cand, y)

    surplus = jnp.max(count(lambda kk, j: kk == thr) - need)
    y = lax.cond(surplus > 0,
                 lambda: lax.fori_loop(0, n_bits, tie_step, jnp.zeros((1, tq), jnp.int32)),
                 lambda: jnp.full((1, tq), 2 ** n_bits - 1, jnp.int32))

    def sel_blk(j, _):
        kk = key_scr[j]
        s = (kk > thr) | ((kk == thr) & (idx_of(j) <= y))
        s = s & (kk > int_min)
        sel_scr[j] = jnp.where(s, 1.0, 0.0).astype(F32)
        return 0

    lax.fori_loop(0, i + 1, sel_blk, 0)

    scale = HEAD_DIM ** -0.5
    q = q_ref[...]
    for c in range(kvh_n):
        qs = jnp.concatenate([q[:, h * HEAD_DIM:(h + 1) * HEAD_DIM] for h in range(c * g, (c + 1) * g)], axis=0)

        def body(j, carry, c=c, qs=qs):
            m, l, acc = carry
            s = _dot_nt(kb_scr[c * nb + j], qs) * scale
            sel = jnp.concatenate([sel_scr[j]] * g, axis=1) > 0.5
            m_new = jnp.maximum(m, jnp.max(jnp.where(sel, s, -1e30), axis=0, keepdims=True))
            p = jnp.where(sel, jnp.exp(s - m_new), 0.0)
            alpha = jnp.exp(m - m_new)
            l = alpha * l + jnp.sum(p, axis=0, keepdims=True)
            acc = alpha * acc + _dot(vt_scr[c * nb + j], p.astype(BF16))
            return m_new, l, acc

        def body2(jj, carry, c=c, qs=qs):
            m, l, acc = carry
            j0 = c * nb + 2 * jj
            s = _dot_nt(jnp.concatenate([kb_scr[j0], kb_scr[j0 + 1]], axis=0), qs) * scale
            sel2 = jnp.concatenate([sel_scr[2 * jj], sel_scr[2 * jj + 1]], axis=0)
            sel = jnp.concatenate([sel2] * g, axis=1) > 0.5
            m_new = jnp.maximum(m, jnp.max(jnp.where(sel, s, -1e30), axis=0, keepdims=True))
            p = jnp.where(sel, jnp.exp(s - m_new), 0.0)
            alpha = jnp.exp(m - m_new)
            l = alpha * l + jnp.sum(p, axis=0, keepdims=True)
            vt2 = jnp.concatenate([vt_scr[j0], vt_scr[j0 + 1]], axis=1)
            acc = alpha * acc + _dot(vt2, p.astype(BF16))
            return m_new, l, acc

        init = (jnp.full((1, g * tq), -1e30, F32), jnp.zeros((1, g * tq), F32),
                jnp.zeros((HEAD_DIM, g * tq), F32))
        n_pair = lax.div(i + 1, 2)
        carry = lax.fori_loop(0, n_pair, body2, init)
        _, l, acc = lax.fori_loop(2 * n_pair, i + 1, body, carry)
        out = acc / l
        for hh in range(g):
            h = c * g + hh
            o_ref[:, h * HEAD_DIM:(h + 1) * HEAD_DIM] = out[:, hh * tq:(hh + 1) * tq].T.astype(o_ref.dtype)


def dsa_prompt(qi, wi, q, ki, k, v, B, T, H, qi_col0=0, q_col0=0):
    tq = 256
    nb = T // tq
    k_sel = min(DSA_TOPK, T // 4)
    kvh_n = k.shape[1] // HEAD_DIM
    wqi = IDX_HEADS * HEAD_DIM
    wq = H * HEAD_DIM
    qib, qb = qi_col0 // wqi, q_col0 // wq
    return pl.pallas_call(
        functools.partial(_dsa_prompt_kernel, nb=nb, k_sel=k_sel),
        grid=(B, nb),
        in_specs=[pl.BlockSpec((tq, wqi), lambda b, i: (b * nb + i, qib)),
                  pl.BlockSpec((tq, LANES), lambda b, i: (b * nb + i, 0)),
                  pl.BlockSpec((tq, wq), lambda b, i: (b * nb + i, qb)),
                  pl.BlockSpec((T, HEAD_DIM), lambda b, i: (b, 0)),
                  pl.BlockSpec((T, k.shape[1]), lambda b, i: (b, 0)),
                  pl.BlockSpec((T, v.shape[1]), lambda b, i: (b, 0))],
        out_specs=pl.BlockSpec((tq, wq), lambda b, i: (b * nb + i, 0)),
        out_shape=jax.ShapeDtypeStruct((B * T, wq), BF16),
        scratch_shapes=[pltpu.VMEM((nb, tq, HEAD_DIM), BF16),
                        pltpu.VMEM((kvh_n * nb, tq, HEAD_DIM), BF16),
                        pltpu.VMEM((kvh_n * nb, HEAD_DIM, tq), BF16),
                        pltpu.VMEM((nb, tq, tq), jnp.int32),
                        pltpu.VMEM((nb, tq, tq), F32)],
        compiler_params=_cparams(("parallel", "arbitrary")),
        name="dsa_prompt",
    )(qi, wi, q, ki, k, v)


def _resident(shape):
    nd = len(shape)
    return pl.BlockSpec(shape, lambda *_: (0,) * nd, pipeline_mode=pl.Buffered(1))


def _merge_kernel(oa_ref, ob_ref, ga_ref, gb_ref, x_ref, wa_ref, wb_ref, wo_ref, o_ref):
    ya = _dot(oa_ref[...], wa_ref[...])
    yb = _dot(ob_ref[...], wb_ref[...])
    z = jax.nn.sigmoid(ga_ref[...].astype(F32)) * ya + jax.nn.sigmoid(gb_ref[...].astype(F32)) * yb
    o_ref[...] = x_ref[...] + _dot(z.astype(BF16), wo_ref[...])


def merge(o_a, o_b, gates, x, w_a, w_b, w_o, tm):
    M, D = x.shape
    return pl.pallas_call(
        _merge_kernel,
        grid=(M // tm,),
        in_specs=[pl.BlockSpec((tm, o_a.shape[1]), lambda i: (i, 0)),
                  pl.BlockSpec((tm, o_b.shape[1]), lambda i: (i, 0)),
                  pl.BlockSpec((tm, D), lambda i: (i, 0)),
                  pl.BlockSpec((tm, D), lambda i: (i, 1)),
                  pl.BlockSpec((tm, D), lambda i: (i, 0)),
                  _resident(w_a.shape), _resident(w_b.shape), _resident(w_o.shape)],
        out_specs=pl.BlockSpec((tm, D), lambda i: (i, 0)),
        out_shape=jax.ShapeDtypeStruct((M, D), F32),
        compiler_params=_cparams(("parallel",)),
        name="merge",
    )(o_a, o_b, gates, gates, x, w_a, w_b, w_o)


RANK_NONE = 64


def _top_values(x, n, order=None, exact=True, want_rank=False):
    vals = []
    rank = jnp.full(x.shape, float(RANK_NONE), F32) if want_rank else None
    if exact:
        if order is None:
            order = lax.broadcasted_iota(jnp.int32, x.shape, 0)
        big = jnp.int32(2 ** 30)
        for a in range(n):
            m = jnp.max(x, axis=0, keepdims=True)
            vals.append(m)
            hit = order == jnp.min(jnp.where(x == m, order, big), axis=0, keepdims=True)
            x = jnp.where(hit, NEG_INF, x)
            if want_rank:
                rank = jnp.where(hit, float(a), rank)
        return jnp.concatenate(vals, axis=0), jnp.zeros((1, x.shape[1]), jnp.int32), rank
    y = x
    for a in range(n):
        m = jnp.max(y, axis=0, keepdims=True)
        vals.append(m)
        hit = y == m
        y = jnp.where(hit, NEG_INF, y)
        if want_rank:
            rank = jnp.where(hit, float(a), rank)
    removed = (jnp.sum(jnp.where(y == NEG_INF, 1, 0), axis=0, keepdims=True)
               - jnp.sum(jnp.where(x == NEG_INF, 1, 0), axis=0, keepdims=True))
    return jnp.concatenate(vals, axis=0), jnp.abs(removed - n), rank


def _pair_candidates(v1, v2):
    k = PEER_TOPK
    tm = v1.shape[1]
    sub = lax.broadcasted_iota(jnp.int32, (8, tm), 0)
    vals = [v1[0:1, :] + v2]
    order = [lax.broadcasted_iota(jnp.int32, (k, tm), 0)]
    for a in range(1, 8):
        vals.append(jnp.where(sub < k // (a + 1), v1[a:a + 1, :] + v2[0:8, :], NEG_INF))
        order.append(sub + a * k)
    vals.append(v1[8:16, :] + v2[0:1, :])
    order.append((sub + 8) * k)
    return jnp.concatenate(vals, axis=0), jnp.concatenate(order, axis=0)


def _peer_topk_kernel(x_ref, g_ref, wq_ref, k1_ref, k2_ref, ht_ref, n1_ref, e1_ref, r2_ref, e2_ref, q_scr):
    tm = x_ref.shape[0]
    hn = x_ref[...]
    hn = _rms(hn, g_ref[...])
    ht_ref[...] = hn.T.astype(BF16)
    q = _dot(hn.astype(BF16), wq_ref[...]).astype(BF16)
    kd = 2 * PEER_N_KEYS
    for hd in range(PEER_HEADS):
        q_scr[hd] = q[:, hd * kd:(hd + 1) * kd]

    def head(hd, _):
        qh = q_scr[hd]
        s1 = _dot_nt(k1_ref[hd], qh[:, :PEER_N_KEYS])
        s2 = _dot_nt(k2_ref[hd], qh[:, PEER_N_KEYS:])

        def gating(exact):
            outs, bad = [[], [], [], []], jnp.zeros((1, LANES), jnp.int32)
            for tc in range(tm // LANES):
                cols = slice(tc * LANES, (tc + 1) * LANES)
                s1c, s2c = s1[:, cols], s2[:, cols]
                v1, b1, rank1 = _top_values(s1c, PEER_TOPK, exact=exact, want_rank=True)
                v2, b2, rank2 = _top_values(s2c, PEER_TOPK, exact=exact, want_rank=True)
                cand, order = _pair_candidates(v1, v2)
                best, b3, _ = _top_values(cand, PEER_TOPK, order, exact=exact)
                thr = best[PEER_TOPK - 1:PEER_TOPK, :]
                inv_z = 1.0 / jnp.sum(jnp.exp(best - best[0:1, :]), axis=0, keepdims=True)
                n1 = jnp.zeros_like(s1c)
                for a in range(PEER_TOPK):
                    n_a = jnp.sum(jnp.where(v1[a:a + 1, :] + v2 >= thr, 1.0, 0.0), axis=0, keepdims=True)
                    n1 = jnp.where(rank1 == float(a), n_a, n1)
                outs[0].append(n1)
                outs[1].append(jnp.exp(s1c - v1[0:1, :]) * inv_z)
                outs[2].append(rank2)
                outs[3].append(jnp.exp(s2c - v2[0:1, :]))
                bad = bad + b1 + b2 + b3
            return tuple(jnp.concatenate(o, axis=1) for o in outs), bad

        fast, bad = gating(False)
        n1, e1, r2, e2 = lax.cond(jnp.max(bad) > 0, lambda: gating(True)[0], lambda: fast)
        n1_ref[hd] = n1
        e1_ref[hd] = e1
        r2_ref[hd] = r2.astype(BF16)
        e2_ref[hd] = e2.astype(BF16)
        return 0

    lax.fori_loop(0, PEER_HEADS, head, 0)


def peer_topk(x, g, w_q, keys1, keys2, tm):
    M, D = x.shape
    hp = PEER_HEADS
    per_key = pl.BlockSpec((hp, PEER_N_KEYS, tm), lambda i: (0, 0, i))
    return pl.pallas_call(
        _peer_topk_kernel,
        grid=(M // tm,),
        in_specs=[pl.BlockSpec((tm, D), lambda i: (i, 0)),
                  pl.BlockSpec((1, D), lambda i: (0, 0)),
                  _resident(w_q.shape), _resident(keys1.shape), _resident(keys2.shape)],
        out_specs=[pl.BlockSpec((D, tm), lambda i: (0, i)), per_key, per_key, per_key, per_key],
        out_shape=[jax.ShapeDtypeStruct((D, M), BF16),
                   jax.ShapeDtypeStruct((hp, PEER_N_KEYS, M), F32),
                   jax.ShapeDtypeStruct((hp, PEER_N_KEYS, M), F32),
                   jax.ShapeDtypeStruct((hp, PEER_N_KEYS, M), BF16),
                   jax.ShapeDtypeStruct((hp, PEER_N_KEYS, M), BF16)],
        scratch_shapes=[pltpu.VMEM((hp, tm, 2 * PEER_N_KEYS), BF16)],
        compiler_params=_cparams(("parallel",)),
        name="peer_topk",
    )(x, g.reshape(1, D), w_q, keys1, keys2)


def _peer_dense_kernel(ht_ref, n1_ref, e1_ref, r2_ref, e2_ref, u_ref, vt_ref, o_ref):
    e = pl.program_id(1)
    n_i1 = u_ref.shape[0] // PEER_N_KEYS

    @pl.when(e == 0)
    def _():
        o_ref[...] = jnp.zeros_like(o_ref)

    a = _dot(u_ref[...], ht_ref[...])
    act = (0.5 * a * (1.0 + lax.erf(a * (2.0 ** -0.5)))).astype(BF16)
    zero = jnp.zeros((), BF16)
    zs = []
    for r in range(n_i1):
        i1 = e * n_i1 + r
        w = None
        for hd in range(PEER_HEADS):
            n_row = n1_ref[hd, pl.ds(i1, 1), :].astype(BF16)
            e_row = e1_ref[hd, pl.ds(i1, 1), :].astype(BF16)
            c = jnp.where(r2_ref[hd] < n_row, e_row * e2_ref[hd], zero)
            w = c if w is None else w + c
        zs.append(act[r * PEER_N_KEYS:(r + 1) * PEER_N_KEYS, :] * w)
    z = jnp.concatenate(zs, axis=0)
    o_ref[...] += _dot(vt_ref[...], z)


def peer_dense(ht, n1, e1, r2, e2, u, vt, tm, e_blk):
    D, M = ht.shape
    E = u.shape[0]
    per_key = pl.BlockSpec((PEER_HEADS, PEER_N_KEYS, tm), lambda i, e: (0, 0, i))
    return pl.pallas_call(
        _peer_dense_kernel,
        grid=(M // tm, E // e_blk),
        in_specs=[pl.BlockSpec((D, tm), lambda i, e: (0, i)), per_key, per_key, per_key, per_key,
                  pl.BlockSpec((e_blk, D), lambda i, e: (e, 0)),
                  pl.BlockSpec((D, e_blk), lambda i, e: (0, e))],
        out_specs=pl.BlockSpec((D, tm), lambda i, e: (0, i)),
        out_shape=jax.ShapeDtypeStruct((D, M), F32),
        compiler_params=_cparams(("parallel", "arbitrary")),
        name="peer_dense",
    )(ht, n1, e1, r2, e2, u, vt)


def _ple_kernel(x_ref, pt_ref, p_ref, g_ref, wg_ref, wp_ref, gf_ref, o_ref, *, final_norm):
    x2 = x_ref[...] + pt_ref[...].T
    h = _rms(x2, g_ref[...]).astype(BF16)
    gate = jax.nn.sigmoid(_dot(h, wg_ref[...]))
    x3 = x2 + gate * _dot(p_ref[...].astype(BF16), wp_ref[...])
    if final_norm:
        x3 = _rms(x3, gf_ref[...])
    o_ref[...] = x3


def ple(x, peer_t, p, g, w_gate, w_proj, g_final, tm, final_norm):
    M, D = x.shape
    return pl.pallas_call(
        functools.partial(_ple_kernel, final_norm=final_norm),
        grid=(M // tm,),
        in_specs=[pl.BlockSpec((tm, D), lambda i: (i, 0)),
                  pl.BlockSpec((D, tm), lambda i: (0, i)),
                  pl.BlockSpec((tm, p.shape[1]), lambda i: (i, 0)),
                  pl.BlockSpec((1, D), lambda i: (0, 0)),
                  _resident(w_gate.shape), _resident(w_proj.shape),
                  pl.BlockSpec((1, D), lambda i: (0, 0))],
        out_specs=pl.BlockSpec((tm, D), lambda i: (i, 0)),
        out_shape=jax.ShapeDtypeStruct((M, D), F32),
        compiler_params=_cparams(("parallel",)),
        name="ple",
    )(x, peer_t, p, g.reshape(1, D), w_gate, w_proj, g_final.reshape(1, D))


SAMPLE_PPS = 8
SCAN_SLOTS = 4
NBP_PAD = 128


def _sample_scan_kernel(pt_ref, qi_ref, wi_ref, kin_ref, qa_ref, cak_ref, cki_ref, isc_ref, inew_ref, ids_ref,
                        ak_buf, ki_buf, sums_scr, sem, *, layer, pps, ps, nbp, n_chunks):
    b = pl.program_id(0)
    n_seq = pl.num_programs(0)
    kvh_n = sums_scr.shape[0]
    ppb = MOBA_BLOCK // ps
    qi = qi_ref[...]
    w = wi_ref[...]

    def copies(seq, ch, slot):
        out = []
        for r in range(pps):
            phys = pt_ref[seq, ch * pps + r]
            out.append(pltpu.make_async_copy(cak_ref.at[layer, phys], ak_buf.at[slot, r], sem.at[0, slot]))
            out.append(pltpu.make_async_copy(cki_ref.at[layer, phys], ki_buf.at[slot, r], sem.at[1, slot]))
        return out

    ahead = SCAN_SLOTS - 1

    @pl.when(b == 0)
    def _():
        for ch0 in range(ahead):
            for c in copies(0, ch0, ch0):
                c.start()

    sums_scr[...] = jnp.zeros_like(sums_scr)
    kin = kin_ref[...].astype(BF16).astype(F32)
    r_new = jnp.sum(qi.astype(F32) * kin, axis=1, keepdims=True)
    inew_ref[...] = jnp.broadcast_to(jnp.sum(jnp.maximum(r_new, 0.0) * w, axis=0, keepdims=True), inew_ref.shape)

    def chunk(ch, _):
        slot = lax.rem(ch, SCAN_SLOTS)
        nxt = ch + ahead
        nxt_slot = lax.rem(nxt, SCAN_SLOTS)

        @pl.when(nxt < n_chunks)
        def _():
            for c in copies(b, nxt, nxt_slot):
                c.start()

        @pl.when((nxt >= n_chunks) & (b + 1 < n_seq))
        def _():
            for c in copies(b + 1, nxt - n_chunks, nxt_slot):
                c.start()

        for c in copies(b, ch, slot):
            c.wait()
        for r in range(pps):
            x = ak_buf[slot, r]
            n = ps
            while n > 1:
                n //= 2
                x = x[:n] + x[n:]
            blk = lax.div(ch * pps + r, ppb)
            for c in range(kvh_n):
                sums_scr[c, pl.ds(blk, 1), :] += x[0, c:c + 1, :]
            sc = _dot_nt(qi, ki_buf[slot, r].astype(BF16))
            isc_ref[pl.ds(ch * pps + r, 1), :] = jnp.sum(jnp.maximum(sc, 0.0) * w, axis=0, keepdims=True)
        return 0

    lax.fori_loop(0, n_chunks, chunk, 0)

    qa = qa_ref[...].astype(F32)
    h_n = qa.shape[0]
    g = h_n // kvh_n
    row = lax.broadcasted_iota(jnp.int32, (h_n, NBP_PAD), 0)
    lane = lax.broadcasted_iota(jnp.int32, (h_n, NBP_PAD), 1)
    gate = jnp.full((h_n, NBP_PAD), NEG_INF, F32)
    for c in range(kvh_n):
        gc = _dot_nt(qa, sums_scr[c] / MOBA_BLOCK, precision=lax.Precision.HIGHEST)
        gate = jnp.where((row >= c * g) & (row < (c + 1) * g), gc, gate)
    gate = jnp.where(lane < nbp, gate, NEG_INF)
    ids = jnp.zeros((h_n, NBP_PAD), jnp.int32)
    for r in range(MOBA_TOPK):
        m = jnp.max(gate, axis=1, keepdims=True)
        first = jnp.min(jnp.where(gate == m, lane, NBP_PAD), axis=1, keepdims=True)
        ids = jnp.where(lane == r, first, ids)
        gate = jnp.where(lane == first, NEG_INF, gate)
    ids_ref[...] = ids


def sample_scan(page_table, cache_a_k, cache_b_kidx, layer, qi, wi, ki_new, qa):
    bs, n_pages = page_table.shape
    ps, kvh_n = cache_a_k.shape[2], cache_a_k.shape[3]
    pps = SAMPLE_PPS
    nbp = n_pages * ps // MOBA_BLOCK
    h_n, hi_n = qa.shape[1], qi.shape[1]
    per_seq = lambda shape: pl.BlockSpec((None,) + shape, lambda b, pt: (b,) + (0,) * len(shape))
    hbm = pl.BlockSpec(memory_space=pl.ANY)
    grid_spec = pltpu.PrefetchScalarGridSpec(
        num_scalar_prefetch=1,
        grid=(bs,),
        in_specs=[per_seq((hi_n, HEAD_DIM)), per_seq((hi_n, 1)), per_seq((1, HEAD_DIM)), per_seq((h_n, HEAD_DIM)),
                  hbm, hbm],
        out_specs=[per_seq((n_pages, ps)), per_seq((1, LANES)), per_seq((h_n, NBP_PAD))],
        scratch_shapes=[pltpu.VMEM((SCAN_SLOTS, pps, ps, kvh_n, HEAD_DIM), F32),
                        pltpu.VMEM((SCAN_SLOTS, pps, ps, HEAD_DIM), F32),
                        pltpu.VMEM((kvh_n, NBP_PAD, HEAD_DIM), F32),
                        pltpu.SemaphoreType.DMA((2, SCAN_SLOTS))],
    )
    return pl.pallas_call(
        functools.partial(_sample_scan_kernel, layer=layer, pps=pps, ps=ps, nbp=nbp, n_chunks=n_pages // pps),
        grid_spec=grid_spec,
        out_shape=[jax.ShapeDtypeStruct((bs, n_pages, ps), F32),
                   jax.ShapeDtypeStruct((bs, 1, LANES), F32),
                   jax.ShapeDtypeStruct((bs, h_n, NBP_PAD), jnp.int32)],
        compiler_params=_cparams(("arbitrary",)),
        name="sample_scan",
    )(page_table, qi, wi, ki_new, qa, cache_a_k, cache_b_kidx)


def _sample_select_kernel(isc_ref, inew_ref, pos_ref, sel_scr, new_scr, rank_scr, *, k_sel, past):
    bs, n_pages, ps = isc_ref.shape
    int_min = jnp.int32(-2 ** 31)
    keys = _ordered_key(isc_ref[...])
    knew = _ordered_key(inew_ref[...][:, :, 0:1])
    idx = (lax.broadcasted_iota(jnp.int32, keys.shape, 1) * ps
           + lax.broadcasted_iota(jnp.int32, keys.shape, 2))

    def count(m_cache, m_new):
        c = jnp.sum(m_cache.astype(jnp.int32), axis=1, keepdims=True)
        return jnp.sum(c, axis=2, keepdims=True) + m_new.astype(jnp.int32)

    def bit_step(it, t_u):
        cand_u = t_u | lax.shift_left(jnp.int32(1), 31 - it)
        cand = cand_u ^ int_min
        return jnp.where(count(keys >= cand, knew >= cand) >= k_sel, cand_u, t_u)

    thr = lax.fori_loop(0, 32, bit_step, jnp.zeros((bs, 1, 1), jnp.int32)) ^ int_min
    need = k_sel - count(keys > thr, knew > thr)
    n_bits = int(math.ceil(math.log2(past + 1)))

    def tie_step(it, y):
        cand = y | lax.shift_left(jnp.int32(1), n_bits - 1 - it)
        c = count((keys == thr) & (idx < cand), (knew == thr) & (past < cand))
        return jnp.where(c < need, cand, y)

    y = lax.fori_loop(0, n_bits, tie_step, jnp.zeros((bs, 1, 1), jnp.int32))
    sel_scr[...] = jnp.where((keys > thr) | ((keys == thr) & (idx <= y)), 1.0, 0.0).astype(F32)
    new_sel = (knew > thr) | ((knew == thr) & (past <= y))
    new_scr[...] = jnp.broadcast_to(jnp.where(new_sel, 1.0, 0.0).astype(F32), new_scr.shape)

    upper = (lax.broadcasted_iota(jnp.int32, (ps, ps), 0) < lax.broadcasted_iota(jnp.int32, (ps, ps), 1))
    lower = (lax.broadcasted_iota(jnp.int32, (n_pages, n_pages), 1)
             < lax.broadcasted_iota(jnp.int32, (n_pages, n_pages), 0))
    upper = jnp.where(upper, 1.0, 0.0).astype(BF16)
    lower = jnp.where(lower, 1.0, 0.0).astype(BF16)
    r_iota = lax.broadcasted_iota(jnp.int32, (k_sel, ps), 0).astype(F32)
    slot = lax.broadcasted_iota(jnp.int32, (k_sel, ps), 1).astype(F32)
    lane = lax.broadcasted_iota(jnp.int32, (1, k_sel), 1)

    def per_seq(b, _):
        sb = sel_scr[b]
        prefix = _dot(sb.astype(BF16), upper)
        cnt = jnp.sum(sb, axis=1, keepdims=True)
        offs = _dot(lower, jnp.broadcast_to(cnt, sb.shape).astype(BF16))
        rank_scr[...] = offs + prefix

        def per_page(p, acc):
            hit = (r_iota == rank_scr[pl.ds(p, 1), :]) & (sel_scr[b, pl.ds(p, 1), :] > 0.5)
            return acc + jnp.where(hit, lax.convert_element_type(p * ps, F32) + slot, 0.0)

        acc = lax.fori_loop(0, n_pages, per_page, jnp.zeros((k_sel, ps), F32))
        tot = jnp.sum(acc.T, axis=0, keepdims=True)
        tot = tot + jnp.where(lane == k_sel - 1, new_scr[b][:, 0:1] * past, 0.0)
        pos_ref[b] = tot.astype(jnp.int32)
        return 0

    lax.fori_loop(0, bs, per_seq, 0)


def sample_select(iscore, inew, k_sel, past):
    bs, n_pages, ps = iscore.shape
    return pl.pallas_call(
        functools.partial(_sample_select_kernel, k_sel=k_sel, past=past),
        out_shape=jax.ShapeDtypeStruct((bs, 1, k_sel), jnp.int32),
        scratch_shapes=[pltpu.VMEM((bs, n_pages, ps), F32),
                        pltpu.VMEM((bs, 1, LANES), F32),
                        pltpu.VMEM((n_pages, ps), F32)],
        compiler_params=pltpu.CompilerParams(vmem_limit_bytes=VMEM_LIMIT),
        name="sample_select",
    )(iscore, inew)


def _sample_attend_kernel(pt_ref, ids_ref, pos_ref, qa_ref, kan_ref, van_ref, qb_ref, kbn_ref, vbn_ref,
                          cak_ref, cav_ref, cbk_ref, cbv_ref, oa_ref, ob_ref,
                          ka_buf, va_buf, kb_buf, vb_buf, sem, *, layer, ps, past, k_sel):
    b = pl.program_id(0)
    n_seq = pl.num_programs(0)
    h_n = qa_ref.shape[0]
    kvh_n = kan_ref.shape[0]
    g = h_n // kvh_n
    ppb = MOBA_BLOCK // ps
    scale = HEAD_DIM ** -0.5
    slot = lax.rem(b, 2)

    def moba_copies(seq, sl, h, r, half):
        phys = pt_ref[seq, ids_ref[seq, h * MOBA_TOPK + r] * ppb + half]
        rows = pl.ds((r * ppb + half) * ps, ps)
        return (pltpu.make_async_copy(cak_ref.at[layer, phys, :, h // g, :], ka_buf.at[sl, h, rows, :], sem.at[0, sl]),
                pltpu.make_async_copy(cav_ref.at[layer, phys, :, h // g, :], va_buf.at[sl, h, rows, :], sem.at[1, sl]))

    def dsa_copies(seq, sl, j):
        s = jnp.minimum(pos_ref[seq, j], past - 1)
        phys = pt_ref[seq, lax.div(s, ps)]
        row = lax.rem(s, ps)
        return (pltpu.make_async_copy(cbk_ref.at[layer, phys, row], kb_buf.at[sl, :, j, :], sem.at[2, sl]),
                pltpu.make_async_copy(cbv_ref.at[layer, phys, row], vb_buf.at[sl, :, j, :], sem.at[3, sl]))

    moba_keys = [(h, r, half) for h in range(h_n) for r in range(MOBA_TOPK) for half in range(ppb)]

    def transfer(seq, sl, start):
        for key in moba_keys:
            for c in moba_copies(seq, sl, *key):
                c.start() if start else c.wait()

        def row(j, _):
            for c in dsa_copies(seq, sl, j):
                c.start() if start else c.wait()
            return 0

        lax.fori_loop(0, k_sel, row, 0)

    @pl.when(b == 0)
    def _():
        transfer(0, 0, True)

    @pl.when(b + 1 < n_seq)
    def _():
        transfer(b + 1, 1 - slot, True)

    transfer(b, slot, False)
    ka_buf, va_buf, kb_buf, vb_buf = ka_buf.at[slot], va_buf.at[slot], kb_buf.at[slot], vb_buf.at[slot]

    @pl.when(pos_ref[b, k_sel - 1] == past)
    def _():
        for c in range(kvh_n):
            kb_buf[c, k_sel - 1:k_sel, :] = kbn_ref[c:c + 1, :]
            vb_buf[c, k_sel - 1:k_sel, :] = vbn_ref[c:c + 1, :]

    qa = qa_ref[...]
    qaf = qa.astype(F32)
    kan = kan_ref[...].astype(BF16).astype(F32)
    van = van_ref[...].astype(BF16).astype(F32)
    for h in range(h_n):
        c = h // g
        s = _dot_nt(qa, ka_buf[h].astype(BF16))[h:h + 1, :] * scale
        s_new = jnp.sum(qaf[h:h + 1, :] * kan[c:c + 1, :], axis=1, keepdims=True) * scale
        m = jnp.maximum(jnp.max(s, axis=1, keepdims=True), s_new)
        p = jnp.exp(s - m)
        p_new = jnp.exp(s_new - m)
        l = jnp.sum(p, axis=1, keepdims=True) + p_new
        pv = _dot(jnp.broadcast_to(p, (8, p.shape[1])).astype(BF16), va_buf[h].astype(BF16))[0:1, :]
        oa_ref[h:h + 1, :] = (pv + p_new * van[c:c + 1, :]) / l

    qb = qb_ref[...]
    for c in range(kvh_n):
        s = _dot_nt(qb, kb_buf[c].astype(BF16)) * scale
        m = jnp.max(s, axis=1, keepdims=True)
        p = jnp.exp(s - m)
        l = jnp.sum(p, axis=1, keepdims=True)
        o = _dot(p.astype(BF16), vb_buf[c].astype(BF16)) / l
        ob_ref[c * g:(c + 1) * g, :] = o[c * g:(c + 1) * g, :]


def sample_attend(page_table, ids, pos, qa, ka_new, va_new, qb, kb_new, vb_new,
                  cache_a_k, cache_a_v, cache_b_k, cache_b_v, layer):
    bs, n_pages = page_table.shape
    ps, kvh_n = cache_a_k.shape[2], cache_a_k.shape[3]
    past = n_pages * ps
    k_sel = pos.shape[1]
    h_n = qa.shape[1]
    per_seq = lambda shape: pl.BlockSpec((None,) + shape, lambda b, *_: (b,) + (0,) * len(shape))
    hbm = pl.BlockSpec(memory_space=pl.ANY)
    grid_spec = pltpu.PrefetchScalarGridSpec(
        num_scalar_prefetch=3,
        grid=(bs,),
        in_specs=[per_seq((h_n, HEAD_DIM)), per_seq((kvh_n, HEAD_DIM)), per_seq((kvh_n, HEAD_DIM)),
                  per_seq((h_n, HEAD_DIM)), per_seq((kvh_n, HEAD_DIM)), per_seq((kvh_n, HEAD_DIM)),
                  hbm, hbm, hbm, hbm],
        out_specs=[per_seq((h_n, HEAD_DIM)), per_seq((h_n, HEAD_DIM))],
        scratch_shapes=[pltpu.VMEM((2, h_n, MOBA_TOPK * MOBA_BLOCK, HEAD_DIM), F32),
                        pltpu.VMEM((2, h_n, MOBA_TOPK * MOBA_BLOCK, HEAD_DIM), F32),
                        pltpu.VMEM((2, kvh_n, k_sel, HEAD_DIM), F32),
                        pltpu.VMEM((2, kvh_n, k_sel, HEAD_DIM), F32),
                        pltpu.SemaphoreType.DMA((4, 2))],
    )
    return pl.pallas_call(
        functools.partial(_sample_attend_kernel, layer=layer, ps=ps, past=past, k_sel=k_sel),
        grid_spec=grid_spec,
        out_shape=[jax.ShapeDtypeStruct((bs, h_n, HEAD_DIM), F32)] * 2,
        compiler_params=_cparams(("arbitrary",)),
        name="sample_attend",
    )(page_table, ids, pos, qa, ka_new, va_new, qb, kb_new, vb_new, cache_a_k, cache_a_v, cache_b_k, cache_b_v)


def _sample_mixers(qa, ka, va, qb, kb, vb, qi, ki, wi, cache_a_k, cache_a_v, cache_b_k, cache_b_v,
                   cache_b_kidx, layer, page_table):
    bs = qa.shape[0]
    ps = cache_a_k.shape[2]
    n_pages = page_table.shape[1]
    past = n_pages * ps
    assert MOBA_BLOCK % ps == 0 and past % MOBA_BLOCK == 0 and n_pages % (SCAN_SLOTS * SAMPLE_PPS) == 0
    assert MOBA_TOPK <= past // MOBA_BLOCK <= NBP_PAD and SAMPLE_PPS % (MOBA_BLOCK // ps) == 0
    k_sel = min(DSA_TOPK, (past + 1) // 4)
    assert k_sel == DSA_TOPK
    r3 = lambda a: a.reshape(bs, -1, HEAD_DIM)
    wi3 = (wi[:, :IDX_HEADS] * (IDX_HEADS * HEAD_DIM) ** -0.5).reshape(bs, IDX_HEADS, 1)
    iscore, inew, ids = sample_scan(page_table, cache_a_k, cache_b_kidx, layer, r3(qi), wi3,
                                    ki.reshape(bs, 1, HEAD_DIM), r3(qa))
    pos = sample_select(iscore, inew, k_sel, past).reshape(bs, k_sel)
    ids = ids[:, :, :MOBA_TOPK].reshape(bs, -1)
    o_a, o_b = sample_attend(page_table, ids, pos, r3(qa), r3(ka), r3(va), r3(qb), r3(kb), r3(vb),
                             cache_a_k, cache_a_v, cache_b_k, cache_b_v, layer)
    return o_a.reshape(bs, -1), o_b.reshape(bs, -1)


N_HEADS_A = 8
N_KV_A = 4
N_HEADS_B = 8
N_KV_B = 4
PLE_PAD_ROWS = 128


def _split_w_in(w):
    d = HEAD_DIM
    sizes = (N_HEADS_A * d, N_KV_A * d, N_KV_A * d, N_HEADS_B * d, N_KV_B * d, N_KV_B * d,
             IDX_HEADS * d, d, IDX_HEADS, w.shape[0], w.shape[0])
    offs = np.concatenate([[0], np.cumsum(sizes)])
    qa, ka, va, qb, kb, vb, qi, ki, wi, ga, gb = (w[:, offs[n]:offs[n + 1]] for n in range(len(sizes)))
    wi = jnp.pad(wi, ((0, 0), (0, LANES - IDX_HEADS)))
    c = lambda *a: jnp.concatenate(a, axis=1).astype(BF16)
    return dict(qq=c(qi, qa, qb), gates=c(ga, gb), ka=c(ka), va=c(va), kb=c(kb), vb=c(vb), ki=c(ki), wi=c(wi))


def _project_all(h, w, cos, sin, tm):
    kw = dict(tm=tm, tn=512)
    return dict(
        qq=project(h, w['qq'], cos, sin, rope=True, out_dtype=BF16, **kw),
        gates=project(h, w['gates'], cos, sin, rope=False, out_dtype=BF16, **kw),
        ka=project(h, w['ka'], cos, sin, rope=True, out_dtype=F32, **kw),
        va=project(h, w['va'], cos, sin, rope=False, out_dtype=F32, **kw),
        kb=project(h, w['kb'], cos, sin, rope=True, out_dtype=F32, **kw),
        vb=project(h, w['vb'], cos, sin, rope=False, out_dtype=F32, **kw),
        ki=project(h, w['ki'], cos, sin, rope=True, out_dtype=F32, **kw),
        wi=project(h, w['wi'], cos, sin, rope=False, out_dtype=F32, **kw),
    )


def _post_mixer(x1, p, lw, g_final, final_norm, tm_topk, tm_dense, tm_ple):
    ht, n1, e1, r2, e2 = peer_topk(x1, lw['g_ffn'], lw['w_peer_q'], lw['keys1'], lw['keys2'], tm_topk)
    pt = peer_dense(ht, n1, e1, r2, e2, lw['u'], lw['vt'], tm_dense, 1024)
    return ple(x1, pt, p, lw['g_ple'], lw['w_ple_gate'], lw['w_ple_proj'], g_final, tm_ple, final_norm)


def kernel(x_prompt, x_sample, cache_a_k, cache_a_v, cache_b_k, cache_b_v, cache_b_kidx, page_table,
           p_prompt, p_sample, g_attn, w_in, w_branch_a, w_branch_b, w_out, g_ffn, w_peer_q,
           peer_keys1, peer_keys2, peer_u, peer_v, g_ple, w_ple_gate, w_ple_proj, g_final):
    B, T, D = x_prompt.shape
    bs, ts, _ = x_sample.shape
    assert ts == 1
    depth = w_in.shape[0]
    past = page_table.shape[1] * cache_a_k.shape[2]
    mp = B * T
    ms = PLE_PAD_ROWS
    qi_w = IDX_HEADS * HEAD_DIM
    qa_w = N_HEADS_A * HEAD_DIM

    cos_p, sin_p = rope_tables(jnp.arange(T, dtype=jnp.int32))
    cos_p, sin_p = jnp.tile(cos_p, (B, 1)), jnp.tile(sin_p, (B, 1))
    cos_s, sin_s = rope_tables(jnp.full((ms,), past, jnp.int32))

    xp = x_prompt.reshape(mp, D)
    xs = jnp.pad(x_sample.reshape(bs, D), ((0, ms - bs), (0, 0)))
    rows_p = [[] for _ in range(5)]
    rows_s = [[] for _ in range(5)]
    for i in range(depth):
        w = _split_w_in(w_in[i])
        lw = dict(g_ffn=g_ffn[i], w_peer_q=w_peer_q[i].astype(BF16), keys1=peer_keys1[i].astype(BF16),
                  keys2=peer_keys2[i].astype(BF16), u=cast_bf16(peer_u[i], 1024), vt=cast_bf16(peer_v[i], 1024, transpose=True),
                  g_ple=g_ple[i], w_ple_gate=w_ple_gate[i].astype(BF16), w_ple_proj=w_ple_proj[i].astype(BF16))
        w_a, w_b, w_o = w_branch_a[i].astype(BF16), w_branch_b[i].astype(BF16), w_out[i].astype(BF16)
        last = i == depth - 1

        sr = _project_all(rmsnorm_bf16(xs, g_attn[i], ms), w, cos_s, sin_s, ms)
        f = lambda a: a[:bs]
        qq = f(sr['qq'])
        so_a, so_b = _sample_mixers(
            qq[:, qi_w:qi_w + qa_w], f(sr['ka']), f(sr['va']), qq[:, qi_w + qa_w:], f(sr['kb']), f(sr['vb']),
            qq[:, :qi_w], f(sr['ki']), f(sr['wi']),
            cache_a_k, cache_a_v, cache_b_k, cache_b_v, cache_b_kidx, i, page_table)
        padr = lambda a: jnp.pad(a, ((0, ms - bs), (0, 0))).astype(BF16)
        x1s = merge(padr(so_a), padr(so_b), sr['gates'], xs, w_a, w_b, w_o, ms)
        ps = jnp.pad(p_sample[i].reshape(bs, -1), ((0, ms - bs), (0, 0)))
        xs = _post_mixer(x1s, ps, lw, g_final, last, ms, ms, ms)
        for lst, name in zip(rows_s, ('ka', 'va', 'kb', 'vb', 'ki')):
            r = sr[name][:bs]
            lst.append(r.reshape((bs, 1, -1, HEAD_DIM) if name != 'ki' else (bs, 1, HEAD_DIM)))

        pr = _project_all(rmsnorm_bf16(xp, g_attn[i], 512), w, cos_p, sin_p, 1024)
        o_a = moba_prompt(pr['qq'], pr['ka'], pr['va'], B, T, N_HEADS_A, q_col0=qi_w)
        o_b = dsa_prompt(pr['qq'], pr['wi'], pr['qq'], pr['ki'], pr['kb'], pr['vb'], B, T, N_HEADS_B,
                         qi_col0=0, q_col0=qi_w + qa_w)
        x1 = merge(o_a, o_b, pr['gates'], xp, w_a, w_b, w_o, 256)
        xp = _post_mixer(x1, p_prompt[i].reshape(mp, -1), lw, g_final, last, 256, 512, 256)
        for lst, name in zip(rows_p, ('ka', 'va', 'kb', 'vb', 'ki')):
            lst.append(pr[name].reshape((B, T, -1, HEAD_DIM) if name != 'ki' else (B, T, HEAD_DIM)))

    y_prompt = xp.reshape(B, T, D)
    y_sample = xs[:bs].reshape(bs, 1, D)
    return (y_prompt, y_sample) + tuple(jnp.stack(r) for r in rows_p) + tuple(jnp.stack(r) for r in rows_s)
```
